```python
import math
import jax, jax.numpy as jnp
from jax import lax
import numpy as np

D_MODEL = 1024
BATCH = 1
SEQ = 16384
DEPTH = 2
DEC_BATCH = 2
DEC_SEQ = 16384
PAST_LEN = 128

CONV_WIDTH = 512
CONV_K = 3
ATTN_HEADS = 8
HEAD_DIM = 64
ATTN_WIDTH = ATTN_HEADS * HEAD_DIM
ROPE_DIM = HEAD_DIM // 4
ROPE_THETA = 500000.0
DILATED_BRANCHES = ((128, 1), (512, 4), (2048, 16))
IN0_WIDTH = 3 * CONV_WIDTH + 3 * ATTN_WIDTH
LRU_WIDTH = D_MODEL
LRU_BLOCKS = 4
LRU_BLOCK_DIM = LRU_WIDTH // LRU_BLOCKS
LRU_CONV_K = 4
LRU_C = 8.0
D_FF = 2816
EPS = 1e-6
MASK_VALUE = -1e30

kernel_name = 'hybrid_conv_dilattn_rglru_macaron_encoder'


def _rmsnorm(x, g):
    xf = x.astype(jnp.float32)
    y = xf * lax.rsqrt(jnp.mean(xf * xf, axis=-1, keepdims=True) + EPS)
    return (y * g.astype(jnp.float32)).astype(x.dtype)


def _swiglu(x, w_gate, w_up, w_down):
    return (jax.nn.silu(x @ w_gate) * (x @ w_up)) @ w_down


def _depthwise_conv(u, w, left):
    K = w.shape[0]
    S = u.shape[1]
    up = jnp.pad(u, ((0, 0), (left, K - 1 - left), (0, 0)))
    y = up[:, 0:S] * w[0]
    for j in range(1, K):
        y = y + up[:, j:j + S] * w[j]
    return y


def _partial_rope(x, pos):
    half = ROPE_DIM // 2
    inv_freq = ROPE_THETA ** (-jnp.arange(half, dtype=jnp.float32) / half)
    ang = pos.astype(jnp.float32)[:, None] * inv_freq[None, :]
    cos = jnp.cos(ang)[None, :, None, :].astype(x.dtype)
    sin = jnp.sin(ang)[None, :, None, :].astype(x.dtype)
    x1 = x[..., :half]
    x2 = x[..., half:ROPE_DIM]
    return jnp.concatenate([x1 * cos - x2 * sin, x2 * cos + x1 * sin, x[..., ROPE_DIM:]], axis=-1)


def _band_attention(q, k, v, half_window):
    N, L, H, Dh = q.shape
    blk = half_window
    nb = -(-L // blk)
    Lp = nb * blk
    qb = jnp.pad(q, ((0, 0), (0, Lp - L), (0, 0), (0, 0))).reshape(N, nb, blk, H, Dh)
    padk = ((0, 0), (blk, Lp - L + blk), (0, 0), (0, 0))
    kr = jnp.pad(k, padk).reshape(N, nb + 2, blk, H, Dh)
    vr = jnp.pad(v, padk).reshape(N, nb + 2, blk, H, Dh)
    kb = jnp.concatenate([kr[:, :-2], kr[:, 1:-1], kr[:, 2:]], axis=2)
    vb = jnp.concatenate([vr[:, :-2], vr[:, 1:-1], vr[:, 2:]], axis=2)
    s = jnp.einsum('nbqhd,nbkhd->nbhqk', qb, kb,
                   preferred_element_type=jnp.float32) * (Dh ** -0.5)
    qpos = jnp.arange(nb)[:, None, None] * blk + jnp.arange(blk)[None, :, None]
    kpos = jnp.arange(nb)[:, None, None] * blk - blk + jnp.arange(3 * blk)[None, None, :]
    valid = (jnp.abs(kpos - qpos) <= half_window) & (kpos >= 0) & (kpos < L)
    s = jnp.where(valid[None, :, None], s, MASK_VALUE)
    m = jnp.max(s, axis=-1, keepdims=True)
    p = jnp.exp(s - m)
    den = jnp.sum(p, axis=-1, keepdims=True)
    o = jnp.einsum('nbhqk,nbkhd->nbqhd', (p / den).astype(v.dtype), vb)
    lse = (m + jnp.log(den))[..., 0]
    o = o.reshape(N, Lp, H, Dh)[:, :L]
    lse = jnp.transpose(lse, (0, 1, 3, 2)).reshape(N, Lp, H)[:, :L]
    return o, lse


def _dilated_attention(q, k, v):
    B, S, H, Dh = q.shape
    outs, lses = [], []
    for window, dil in DILATED_BRANCHES:
        L = S // dil

        def split(t):
            return t.reshape(B, L, dil, H, Dh).transpose(0, 2, 1, 3, 4).reshape(B * dil, L, H, Dh)

        o, lse = _band_attention(split(q), split(k), split(v), window // (2 * dil))
        outs.append(o.reshape(B, dil, L, H, Dh).transpose(0, 2, 1, 3, 4).reshape(B, S, H, Dh))
        lses.append(lse.reshape(B, dil, L, H).transpose(0, 2, 1, 3).reshape(B, S, H))
    w = jax.nn.softmax(jnp.stack(lses, axis=-1), axis=-1)
    o = w[..., 0:1].astype(q.dtype) * outs[0]
    for g in range(1, len(outs)):
        o = o + w[..., g:g + 1].astype(q.dtype) * outs[g]
    return o


def _conv_attn_mixer(h, pos, w_in, conv_w, w_out):
    B, S, _ = h.shape
    z = h @ w_in
    c, a = CONV_WIDTH, ATTN_WIDTH
    u, gb, gc, q, k, v = jnp.split(z, [c, 2 * c, 3 * c, 3 * c + a, 3 * c + 2 * a], axis=-1)
    ya = gb * _depthwise_conv(gc * u, conv_w, left=1)
    q = _partial_rope(q.reshape(B, S, ATTN_HEADS, HEAD_DIM), pos)
    k = _partial_rope(k.reshape(B, S, ATTN_HEADS, HEAD_DIM), pos)
    v = v.reshape(B, S, ATTN_HEADS, HEAD_DIM)
    yb = _dilated_attention(q, k, v).reshape(B, S, ATTN_WIDTH)
    return jnp.concatenate([ya, yb], axis=-1) @ w_out


def _rglru_direction(xb, w_a, b_a, w_i, b_i, lam, reverse):
    B, S, W = xb.shape
    xg = xb.reshape(B, S, LRU_BLOCKS, LRU_BLOCK_DIM)
    r = jax.nn.sigmoid((jnp.einsum('bsgi,gij->bsgj', xg, w_a).reshape(B, S, W) + b_a).astype(jnp.float32))
    i = jax.nn.sigmoid((jnp.einsum('bsgi,gij->bsgj', xg, w_i).reshape(B, S, W) + b_i).astype(jnp.float32))
    log_a = -LRU_C * r * jax.nn.softplus(-lam.astype(jnp.float32))
    a = jnp.exp(log_a)
    bterm = jnp.sqrt(-jnp.expm1(2.0 * log_a)) * i * xb.astype(jnp.float32)

    def combine(c1, c2):
        a1, b1 = c1
        a2, b2 = c2
        return a1 * a2, a2 * b1 + b2

    _, hseq = lax.associative_scan(combine, (a, bterm), reverse=reverse, axis=1)
    return hseq


def _rglru_mixer(h, w_in, conv_w, conv_b,
                 fwd_w_a, fwd_b_a, fwd_w_i, fwd_b_i, fwd_lambda,
                 bwd_w_a, bwd_b_a, bwd_w_i, bwd_b_i, bwd_lambda, w_out):
    z = h @ w_in
    xb, gate = jnp.split(z, [LRU_WIDTH], axis=-1)
    xb = _depthwise_conv(xb, conv_w, left=2) + conv_b
    hf = _rglru_direction(xb, fwd_w_a, fwd_b_a, fwd_w_i, fwd_b_i, fwd_lambda, reverse=False)
    hb = _rglru_direction(xb, bwd_w_a, bwd_b_a, bwd_w_i, bwd_b_i, bwd_lambda, reverse=True)
    y = (hf + hb).astype(h.dtype) * jax.nn.gelu(gate)
    return y @ w_out


def _trunk(x, layers, final_norm):
    pos = jnp.arange(x.shape[1], dtype=jnp.int32)
    for layer in range(DEPTH):
        p = layers[layer]
        x = x + 0.5 * _swiglu(_rmsnorm(x, p['ffn1_norm']), p['ffn1_w_gate'], p['ffn1_w_up'], p['ffn1_w_down'])
        h = _rmsnorm(x, p['mix_norm'])
        if layer % 2 == 0:
            x = x + _conv_attn_mixer(h, pos, p['w_in'], p['conv_w'], p['w_out'])
        else:
            x = x + _rglru_mixer(h, p['w_in'], p['conv_w'], p['conv_b'],
                                 p['fwd_w_a'], p['fwd_b_a'], p['fwd_w_i'], p['fwd_b_i'], p['fwd_lambda'],
                                 p['bwd_w_a'], p['bwd_b_a'], p['bwd_w_i'], p['bwd_b_i'], p['bwd_lambda'],
                                 p['w_out'])
        x = x + 0.5 * _swiglu(_rmsnorm(x, p['ffn2_norm']), p['ffn2_w_gate'], p['ffn2_w_up'], p['ffn2_w_down'])
    return _rmsnorm(x, final_norm)


def setup_inputs(seed: int = 0) -> dict:
    key = jax.random.key(seed)
    keys = jax.random.split(key, 64)
    counter = [0]

    def nk():
        counter[0] += 1
        return keys[counter[0] - 1]

    def dense(shape, fan_in):
        return jax.random.normal(nk(), shape, jnp.float32) * fan_in ** -0.5

    def gain(n):
        return 1.0 + 0.05 * jax.random.normal(nk(), (n,), jnp.float32)

    def bias(n):
        return 0.02 * jax.random.normal(nk(), (n,), jnp.float32)

    def lru_lambda(n):
        u = jax.random.uniform(nk(), (n,), jnp.float32, 0.9, 0.999)
        a = u ** (1.0 / LRU_C)
        return jnp.log(a) - jnp.log1p(-a)

    D, F = D_MODEL, D_FF
    inp = {}
    inp['x_prompt'] = jax.random.normal(nk(), (BATCH, SEQ, D), jnp.float32)
    inp['x_sample'] = jax.random.normal(nk(), (DEC_BATCH, DEC_SEQ, D), jnp.float32)
    inp['l0_ffn1_norm'] = gain(D)
    inp['l0_ffn1_w_gate'] = dense((D, F), D)
    inp['l0_ffn1_w_up'] = dense((D, F), D)
    inp['l0_ffn1_w_down'] = dense((F, D), F)
    inp['l0_mix_norm'] = gain(D)
    inp['l0_w_in'] = dense((D, IN0_WIDTH), D)
    inp['l0_conv_w'] = dense((CONV_K, CONV_WIDTH), CONV_K)
    inp['l0_w_out'] = dense((CONV_WIDTH + ATTN_WIDTH, D), CONV_WIDTH + ATTN_WIDTH)
    inp['l0_ffn2_norm'] = gain(D)
    inp['l0_ffn2_w_gate'] = dense((D, F), D)
    inp['l0_ffn2_w_up'] = dense((D, F), D)
    inp['l0_ffn2_w_down'] = dense((F, D), F)
    inp['l1_ffn1_norm'] = gain(D)
    inp['l1_ffn1_w_gate'] = dense((D, F), D)
    inp['l1_ffn1_w_up'] = dense((D, F), D)
    inp['l1_ffn1_w_down'] = dense((F, D), F)
    inp['l1_mix_norm'] = gain(D)
    inp['l1_w_in'] = dense((D, 2 * LRU_WIDTH), D)
    inp['l1_conv_w'] = dense((LRU_CONV_K, LRU_WIDTH), LRU_CONV_K)
    inp['l1_conv_b'] = bias(LRU_WIDTH)
    inp['l1_fwd_w_a'] = dense((LRU_BLOCKS, LRU_BLOCK_DIM, LRU_BLOCK_DIM), LRU_BLOCK_DIM)
    inp['l1_fwd_b_a'] = bias(LRU_WIDTH)
    inp['l1_fwd_w_i'] = dense((LRU_BLOCKS, LRU_BLOCK_DIM, LRU_BLOCK_DIM), LRU_BLOCK_DIM)
    inp['l1_fwd_b_i'] = bias(LRU_WIDTH)
    inp['l1_fwd_lambda'] = lru_lambda(LRU_WIDTH)
    inp['l1_bwd_w_a'] = dense((LRU_BLOCKS, LRU_BLOCK_DIM, LRU_BLOCK_DIM), LRU_BLOCK_DIM)
    inp['l1_bwd_b_a'] = bias(LRU_WIDTH)
    inp['l1_bwd_w_i'] = dense((LRU_BLOCKS, LRU_BLOCK_DIM, LRU_BLOCK_DIM), LRU_BLOCK_DIM)
    inp['l1_bwd_b_i'] = bias(LRU_WIDTH)
    inp['l1_bwd_lambda'] = lru_lambda(LRU_WIDTH)
    inp['l1_w_out'] = dense((LRU_WIDTH, D), LRU_WIDTH)
    inp['l1_ffn2_norm'] = gain(D)
    inp['l1_ffn2_w_gate'] = dense((D, F), D)
    inp['l1_ffn2_w_up'] = dense((D, F), D)
    inp['l1_ffn2_w_down'] = dense((F, D), F)
    inp['final_norm'] = gain(D)
    return inp


def reference(x_prompt, x_sample,
              l0_ffn1_norm, l0_ffn1_w_gate, l0_ffn1_w_up, l0_ffn1_w_down,
              l0_mix_norm, l0_w_in, l0_conv_w, l0_w_out,
              l0_ffn2_norm, l0_ffn2_w_gate, l0_ffn2_w_up, l0_ffn2_w_down,
              l1_ffn1_norm, l1_ffn1_w_gate, l1_ffn1_w_up, l1_ffn1_w_down,
              l1_mix_norm, l1_w_in, l1_conv_w, l1_conv_b,
              l1_fwd_w_a, l1_fwd_b_a, l1_fwd_w_i, l1_fwd_b_i, l1_fwd_lambda,
              l1_bwd_w_a, l1_bwd_b_a, l1_bwd_w_i, l1_bwd_b_i, l1_bwd_lambda,
              l1_w_out,
              l1_ffn2_norm, l1_ffn2_w_gate, l1_ffn2_w_up, l1_ffn2_w_down,
              final_norm):
    layer0 = {
        'ffn1_norm': l0_ffn1_norm, 'ffn1_w_gate': l0_ffn1_w_gate, 'ffn1_w_up': l0_ffn1_w_up,
        'ffn1_w_down': l0_ffn1_w_down, 'mix_norm': l0_mix_norm, 'w_in': l0_w_in,
        'conv_w': l0_conv_w, 'w_out': l0_w_out, 'ffn2_norm': l0_ffn2_norm,
        'ffn2_w_gate': l0_ffn2_w_gate, 'ffn2_w_up': l0_ffn2_w_up, 'ffn2_w_down': l0_ffn2_w_down,
    }
    layer1 = {
        'ffn1_norm': l1_ffn1_norm, 'ffn1_w_gate': l1_ffn1_w_gate, 'ffn1_w_up': l1_ffn1_w_up,
        'ffn1_w_down': l1_ffn1_w_down, 'mix_norm': l1_mix_norm, 'w_in': l1_w_in,
        'conv_w': l1_conv_w, 'conv_b': l1_conv_b,
        'fwd_w_a': l1_fwd_w_a, 'fwd_b_a': l1_fwd_b_a, 'fwd_w_i': l1_fwd_w_i,
        'fwd_b_i': l1_fwd_b_i, 'fwd_lambda': l1_fwd_lambda,
        'bwd_w_a': l1_bwd_w_a, 'bwd_b_a': l1_bwd_b_a, 'bwd_w_i': l1_bwd_w_i,
        'bwd_b_i': l1_bwd_b_i, 'bwd_lambda': l1_bwd_lambda,
        'w_out': l1_w_out, 'ffn2_norm': l1_ffn2_norm,
        'ffn2_w_gate': l1_ffn2_w_gate, 'ffn2_w_up': l1_ffn2_w_up, 'ffn2_w_down': l1_ffn2_w_down,
    }
    layers = [layer0, layer1]
    y_prompt = _trunk(x_prompt, layers, final_norm)
    y_sample = _trunk(x_sample, layers, final_norm)
    return (y_prompt, y_sample)
```

```python
import functools
import math

import jax
import jax.numpy as jnp
from jax import lax
from jax.experimental import pallas as pl
from jax.experimental.pallas import tpu as pltpu

F32 = jnp.float32
BF16 = jnp.bfloat16

D_MODEL = 1024
SEQ = 16384
N_SEQ = 3
T_TOK = N_SEQ * SEQ
D_FF = 2816
EPS = 1e-6
MASK_VALUE = -1e30

CONV_WIDTH = 512
ATTN_HEADS = 8
HEAD_DIM = 64
ATTN_WIDTH = ATTN_HEADS * HEAD_DIM
ROPE_DIM = HEAD_DIM // 4
ROPE_HALF = ROPE_DIM // 2
ROPE_THETA = 500000.0
DILATIONS = (1, 4, 16)
HALF_WINDOW = 64
LRU_WIDTH = 1024
LRU_BLOCKS = 4
LRU_BLOCK_DIM = LRU_WIDTH // LRU_BLOCKS
LRU_C = 8.0

LANES = 128
SUBLANES = 8
BF16_ROWS = 16
VMEM_LIMIT = 56 * 1024 * 1024

TM = 512
TILES_PER_SEQ = SEQ // TM
FF_CHUNKS = ((0, 1024), (1024, 1024), (2048, 768))
TQ = 512
QB = 128
KW = QB + 2 * HALF_WINDOW
TS = 256
SCAN_TILES = SEQ // TS


def _cparams(*sem):
    return pltpu.CompilerParams(dimension_semantics=sem, vmem_limit_bytes=VMEM_LIMIT)


def _const_spec(shape):
    nd = len(shape)
    return pl.BlockSpec(shape, lambda *_: (0,) * nd, pipeline_mode=pl.Buffered(1))


def _rmsnorm(x, g):
    ms = jnp.mean(x * x, axis=-1, keepdims=True)
    return x * lax.rsqrt(ms + EPS) * g


def _sigmoid(x):
    return 1.0 / (1.0 + jnp.exp(-x))


def _ffn_kernel(x_ref, g_ref, wg_ref, wu_ref, wd_ref, fn_ref, o_ref, *, final):
    x = x_ref[...]
    h = _rmsnorm(x, g_ref[...]).astype(BF16)
    acc = None
    for c0, cw in FF_CHUNKS:
        g = jnp.dot(h, wg_ref[:, c0:c0 + cw], preferred_element_type=F32)
        u = jnp.dot(h, wu_ref[:, c0:c0 + cw], preferred_element_type=F32)
        a = (g * _sigmoid(g) * u).astype(BF16)
        y = jnp.dot(a, wd_ref[c0:c0 + cw, :], preferred_element_type=F32)
        acc = y if acc is None else acc + y
    out = x + 0.5 * acc
    if final:
        out = _rmsnorm(out, fn_ref[...])
    o_ref[...] = out


def _ffn(x, g, wg, wu, wd, fn, final):
    row = pl.BlockSpec((TM, D_MODEL), lambda i: (i, 0))
    return pl.pallas_call(
        functools.partial(_ffn_kernel, final=final),
        grid=(T_TOK // TM,),
        in_specs=[row, _const_spec((1, D_MODEL)), _const_spec((D_MODEL, D_FF)), _const_spec((D_MODEL, D_FF)),
                  _const_spec((D_FF, D_MODEL)), _const_spec((1, D_MODEL))],
        out_specs=row,
        out_shape=jax.ShapeDtypeStruct((T_TOK, D_MODEL), F32),
        compiler_params=_cparams("arbitrary"),
        name="ffn_final" if final else "ffn",
    )(x, g, wg, wu, wd, fn)


def _rope(x, cos, sin_lo, sin_hi):
    return (x * cos + pltpu.roll(x, ATTN_WIDTH - ROPE_HALF, axis=1) * sin_lo
            + pltpu.roll(x, ROPE_HALF, axis=1) * sin_hi)


def _inproj0_kernel(x_ref, g_ref, w_ref, freq_ref, p_ref, gb_ref, q_ref, k_ref, v_ref):
    h = _rmsnorm(x_ref[...], g_ref[...]).astype(BF16)
    z = jnp.dot(h, w_ref[...], preferred_element_type=F32)
    c = CONV_WIDTH
    u, gb, gc = z[:, 0:c], z[:, c:2 * c], z[:, 2 * c:3 * c]
    p_ref[...] = (gc * u).astype(BF16)
    gb_ref[...] = gb.astype(BF16)
    a0 = 3 * c
    q = z[:, a0:a0 + ATTN_WIDTH]
    k = z[:, a0 + ATTN_WIDTH:a0 + 2 * ATTN_WIDTH]
    v = z[:, a0 + 2 * ATTN_WIDTH:a0 + 3 * ATTN_WIDTH]

    tile = pl.program_id(0) % TILES_PER_SEQ
    pos = (tile * TM + lax.broadcasted_iota(jnp.int32, (TM, LANES), 0)).astype(F32)
    ang = pos * freq_ref[...]
    cos1, sin1 = jnp.cos(ang), jnp.sin(ang)
    lane = lax.broadcasted_iota(jnp.int32, (TM, LANES), 1)
    low = (lane & ROPE_HALF) == 0
    reps = ATTN_WIDTH // LANES
    cos = jnp.concatenate([cos1] * reps, axis=1)
    sin_lo = jnp.concatenate([jnp.where(low, -sin1, 0.0)] * reps, axis=1)
    sin_hi = jnp.concatenate([jnp.where(low, 0.0, sin1)] * reps, axis=1)
    q_ref[...] = (_rope(q, cos, sin_lo, sin_hi) * (HEAD_DIM ** -0.5)).astype(BF16)
    k_ref[...] = _rope(k, cos, sin_lo, sin_hi).astype(BF16)
    v_ref[...] = v.astype(BF16)


def _inproj0(x, g, w, freq):
    row = pl.BlockSpec((TM, D_MODEL), lambda i: (i, 0))
    half = pl.BlockSpec((TM, CONV_WIDTH), lambda i: (i, 0))
    out = jax.ShapeDtypeStruct((T_TOK, CONV_WIDTH), BF16)
    return pl.pallas_call(
        _inproj0_kernel,
        grid=(T_TOK // TM,),
        in_specs=[row, _const_spec((1, D_MODEL)), _const_spec(w.shape), _const_spec((1, LANES))],
        out_specs=[half] * 5,
        out_shape=[out] * 5,
        compiler_params=_cparams("arbitrary"),
        name="inproj0",
    )(x, g, w, freq)


def _attn_kernel(q_ref, km_ref, kp_ref, kn_ref, vm_ref, vp_ref, vn_ref, o_ref, lse_ref, kw_ref, vw_ref, *, seq_len):
    hw = HALF_WINDOW
    kw_ref[0:hw, :] = kp_ref[...]
    kw_ref[hw:hw + TQ, :] = km_ref[...]
    kw_ref[hw + TQ:, :] = kn_ref[...]
    vw_ref[0:hw, :] = vp_ref[...]
    vw_ref[hw:hw + TQ, :] = vm_ref[...]
    vw_ref[hw + TQ:, :] = vn_ref[...]

    tile0 = pl.program_id(2) * TQ
    row = lax.broadcasted_iota(jnp.int32, (QB, KW), 0)
    col = lax.broadcasted_iota(jnp.int32, (QB, KW), 1)
    head_lane = lax.broadcasted_iota(jnp.int32, (QB, 2 * LANES), 1) // HEAD_DIM
    head_mask = [(head_lane == c).astype(F32).astype(BF16) for c in range(4)]
    stat_lane = lax.broadcasted_iota(jnp.int32, (QB, LANES), 1)

    def body(j, carry):
        q0 = tile0 + j * QB
        off = pl.multiple_of(j * QB, QB)
        lo = jnp.maximum(row, hw - q0)
        hi = jnp.minimum(row + 2 * hw, seq_len - 1 + hw - q0)
        valid = (col >= lo) & (col <= hi)
        qb = q_ref[pl.ds(off, QB), :]
        lse_all = jnp.zeros((QB, LANES), F32)
        for half in range(2):
            lanes = slice(half * 2 * LANES, (half + 1) * 2 * LANES)
            qh_all = qb[:, lanes]
            kwin = kw_ref[pl.ds(off, KW), lanes]
            vwin = vw_ref[pl.ds(off, KW), lanes]
            acc = jnp.zeros((QB, 2 * LANES), F32)
            for c in range(4):
                head = half * 4 + c
                qh = qh_all * head_mask[c]
                s = lax.dot_general(qh, kwin, (((1,), (1,)), ((), ())), preferred_element_type=F32)
                s = jnp.where(valid, s, MASK_VALUE)
                m = jnp.max(s, axis=-1, keepdims=True)
                p = jnp.exp(s - m)
                den = jnp.sum(p, axis=-1, keepdims=True)
                pv = jnp.dot(p.astype(BF16), vwin, preferred_element_type=F32)
                acc = jnp.where(head_lane == c, pv * (1.0 / den), acc)
                lse_all = jnp.where(stat_lane == head, m + jnp.log(den), lse_all)
            o_ref[pl.ds(off, QB), lanes] = acc.astype(BF16)
        lse_ref[pl.ds(off, QB), :] = lse_all
        return carry

    lax.fori_loop(0, TQ // QB, body, 0)


def _attention_branch(q, k, v, dil):
    seq_len = SEQ // dil
    rows = N_SEQ * seq_len
    tiles = seq_len // TQ
    hblk = seq_len // HALF_WINDOW
    view = lambda t: t.reshape(rows, dil * ATTN_WIDTH)
    main = pl.BlockSpec((TQ, ATTN_WIDTH), lambda b, r, j: (b * tiles + j, r))
    prev = pl.BlockSpec((HALF_WINDOW, ATTN_WIDTH),
                        lambda b, r, j: (b * hblk + jnp.maximum(j * (TQ // HALF_WINDOW) - 1, 0), r))
    nxt = pl.BlockSpec((HALF_WINDOW, ATTN_WIDTH),
                       lambda b, r, j: (b * hblk + jnp.minimum((j + 1) * (TQ // HALF_WINDOW), hblk - 1), r))
    stat = pl.BlockSpec((TQ, LANES), lambda b, r, j: (b * tiles + j, r))
    o, lse = pl.pallas_call(
        functools.partial(_attn_kernel, seq_len=seq_len),
        grid=(N_SEQ, dil, tiles),
        in_specs=[main, main, prev, nxt, main, prev, nxt],
        out_specs=[main, stat],
        out_shape=[jax.ShapeDtypeStruct((rows, dil * ATTN_WIDTH), BF16),
                   jax.ShapeDtypeStruct((rows, dil * LANES), F32)],
        scratch_shapes=[pltpu.VMEM((TQ + 2 * HALF_WINDOW, ATTN_WIDTH), BF16),
                        pltpu.VMEM((TQ + 2 * HALF_WINDOW, ATTN_WIDTH), BF16)],
        compiler_params=_cparams("arbitrary", "arbitrary", "arbitrary"),
        name=f"attn_d{dil}",
    )(view(q), view(k), view(k), view(k), view(v), view(v), view(v))
    return o.reshape(T_TOK, ATTN_WIDTH), lse.reshape(T_TOK, LANES)


def _shifted(prev, cur, nxt, shift):
    ext = jnp.concatenate([prev, cur, nxt], axis=0)
    n = ext.shape[0]
    lo = prev.shape[0]
    return pltpu.roll(ext, shift % n, axis=0)[lo:lo + cur.shape[0]]


def _outproj0_kernel(x_ref, p_ref, pp_ref, pn_ref, gb_ref, o1_ref, o2_ref, o3_ref, l1_ref, l2_ref, l3_ref,
                     cw_ref, ex_ref, w_ref, out_ref):
    tile = pl.program_id(0) % TILES_PER_SEQ
    first = (tile == 0)
    last = (tile == TILES_PER_SEQ - 1)
    cur = p_ref[...].astype(F32)
    prev = jnp.where(first, 0.0, pp_ref[...].astype(F32))
    nxt = jnp.where(last, 0.0, pn_ref[...].astype(F32))
    cw = cw_ref[...]
    conv = (cw[0:1, :] * _shifted(prev, cur, nxt, 1) + cw[1:2, :] * cur + cw[2:3, :] * _shifted(prev, cur, nxt, -1))
    ya = gb_ref[...].astype(F32) * conv

    l1, l2, l3 = l1_ref[...], l2_ref[...], l3_ref[...]
    m = jnp.maximum(jnp.maximum(l1, l2), l3)
    e1, e2, e3 = jnp.exp(l1 - m), jnp.exp(l2 - m), jnp.exp(l3 - m)
    tot = e1 + e2 + e3
    yb = None
    for e, o_ref in ((e1, o1_ref), (e2, o2_ref), (e3, o3_ref)):
        wgt = e / tot
        hi = wgt.astype(BF16)
        lo = (wgt - hi.astype(F32)).astype(BF16)
        wide = jnp.dot(jnp.concatenate([hi, lo], axis=1), ex_ref[...], preferred_element_type=F32)
        term = wide * o_ref[...].astype(F32)
        yb = term if yb is None else yb + term
    y = jnp.concatenate([ya.astype(BF16), yb.astype(BF16)], axis=1)
    out_ref[...] = x_ref[...] + jnp.dot(y, w_ref[...], preferred_element_type=F32)


def _outproj0(x, p, gb, outs, lses, conv_w, expand, w):
    row = pl.BlockSpec((TM, D_MODEL), lambda i: (i, 0))
    half = pl.BlockSpec((TM, CONV_WIDTH), lambda i: (i, 0))
    hb = TM // BF16_ROWS
    nblk = T_TOK // BF16_ROWS
    prev = pl.BlockSpec((BF16_ROWS, CONV_WIDTH), lambda i: (jnp.maximum(i * hb - 1, 0), 0))
    nxt = pl.BlockSpec((BF16_ROWS, CONV_WIDTH), lambda i: (jnp.minimum((i + 1) * hb, nblk - 1), 0))
    stat = pl.BlockSpec((TM, LANES), lambda i: (i, 0))
    return pl.pallas_call(
        _outproj0_kernel,
        grid=(T_TOK // TM,),
        in_specs=[row, half, prev, nxt, half, half, half, half, stat, stat, stat,
                  _const_spec(conv_w.shape), _const_spec(expand.shape), _const_spec(w.shape)],
        out_specs=row,
        out_shape=jax.ShapeDtypeStruct((T_TOK, D_MODEL), F32),
        compiler_params=_cparams("arbitrary"),
        name="outproj0",
    )(x, p, p, p, gb, *outs, *lses, conv_w, expand, w)


def _inproj1_kernel(x_ref, g_ref, w_ref, xb_ref, gate_ref):
    h = _rmsnorm(x_ref[...], g_ref[...]).astype(BF16)
    z = jnp.dot(h, w_ref[...], preferred_element_type=F32)
    xb_ref[...] = z[:, :LRU_WIDTH]
    gate = z[:, LRU_WIDTH:]
    inner = math.sqrt(2.0 / math.pi) * (gate + 0.044715 * (gate * gate * gate))
    gate_ref[...] = (0.5 * gate * (1.0 + jnp.tanh(inner))).astype(BF16)


def _inproj1(x, g, w):
    row = pl.BlockSpec((TM, D_MODEL), lambda i: (i, 0))
    return pl.pallas_call(
        _inproj1_kernel,
        grid=(T_TOK // TM,),
        in_specs=[row, _const_spec((1, D_MODEL)), _const_spec(w.shape)],
        out_specs=[row, row],
        out_shape=[jax.ShapeDtypeStruct((T_TOK, LRU_WIDTH), F32), jax.ShapeDtypeStruct((T_TOK, LRU_WIDTH), BF16)],
        compiler_params=_cparams("arbitrary"),
        name="inproj1",
    )(x, g, w)


def _lru_terms(xb, wa_ref, ba_ref, wi_ref, bi_ref, lam_ref):
    xg = xb.astype(BF16)
    ra, ia = [], []
    for g in range(LRU_BLOCKS):
        blk = xg[:, g * LRU_BLOCK_DIM:(g + 1) * LRU_BLOCK_DIM]
        ra.append(jnp.dot(blk, wa_ref[g], preferred_element_type=F32))
        ia.append(jnp.dot(blk, wi_ref[g], preferred_element_type=F32))
    r = _sigmoid(jnp.concatenate(ra, axis=1) + ba_ref[...])
    i = _sigmoid(jnp.concatenate(ia, axis=1) + bi_ref[...])
    z = -lam_ref[...]
    softplus = jnp.maximum(z, 0.0) + jnp.log(1.0 + jnp.exp(-jnp.abs(z)))
    log_a = -LRU_C * r * softplus
    a = jnp.exp(log_a)
    b = jnp.sqrt(1.0 - a * a) * i * xb
    return a, b


def _scan_tile(a, b, carry, reverse):
    groups = TS // SUBLANES
    idx = lax.broadcasted_iota(jnp.int32, a.shape, 0) % SUBLANES
    for s in (1, 2, 4):
        if reverse:
            keep = idx < SUBLANES - s
            shift = TS - s
        else:
            keep = idx >= s
            shift = s
        a_s = jnp.where(keep, pltpu.roll(a, shift, axis=0), 1.0)
        b_s = jnp.where(keep, pltpu.roll(b, shift, axis=0), 0.0)
        b = a * b_s + b
        a = a * a_s
    out = [None] * groups
    order = range(groups - 1, -1, -1) if reverse else range(groups)
    edge = 0 if reverse else SUBLANES - 1
    h = carry
    for g in order:
        rows = slice(g * SUBLANES, (g + 1) * SUBLANES)
        blk = b[rows] + a[rows] * h
        out[g] = blk
        h = blk[edge:edge + 1, :]
    return jnp.concatenate(out, axis=0), h


def _lru_kernel(xf_ref, xfp_ref, xfn_ref, xr_ref, xrp_ref, xrn_ref, cw_ref, cb_ref,
                fwa_ref, fba_ref, fwi_ref, fbi_ref, flam_ref, bwa_ref, bba_ref, bwi_ref, bbi_ref, blam_ref,
                hf_ref, hb_ref, cf_ref, cr_ref):
    i = pl.program_id(1)

    @pl.when(i == 0)
    def _():
        cf_ref[...] = jnp.zeros_like(cf_ref)
        cr_ref[...] = jnp.zeros_like(cr_ref)

    cw = cw_ref[...]

    def conv(cur_ref, prev_ref, next_ref, tile):
        cur = cur_ref[...]
        prev = jnp.where(tile == 0, 0.0, prev_ref[...])
        nxt = jnp.where(tile == SCAN_TILES - 1, 0.0, next_ref[...])
        return (cw[0:1, :] * _shifted(prev, cur, nxt, 2) + cw[1:2, :] * _shifted(prev, cur, nxt, 1)
                + cw[2:3, :] * cur + cw[3:4, :] * _shifted(prev, cur, nxt, -1) + cb_ref[...])

    a, b = _lru_terms(conv(xf_ref, xfp_ref, xfn_ref, i), fwa_ref, fba_ref, fwi_ref, fbi_ref, flam_ref)
    h, carry = _scan_tile(a, b, cf_ref[0:1, :], reverse=False)
    hf_ref[...] = h.astype(BF16)
    cf_ref[0:1, :] = carry

    a, b = _lru_terms(conv(xr_ref, xrp_ref, xrn_ref, SCAN_TILES - 1 - i), bwa_ref, bba_ref, bwi_ref, bbi_ref, blam_ref)
    h, carry = _scan_tile(a, b, cr_ref[0:1, :], reverse=True)
    hb_ref[...] = h.astype(BF16)
    cr_ref[0:1, :] = carry


def _lru(xb, conv_w, conv_b, fwd, bwd):
    hb = TS // SUBLANES
    sblk = SEQ // SUBLANES
    fidx = lambda b, i: b * SCAN_TILES + i
    ridx = lambda b, i: b * SCAN_TILES + SCAN_TILES - 1 - i

    def specs(tile_of):
        local = lambda b, i: tile_of(b, i) - b * SCAN_TILES
        return [pl.BlockSpec((TS, LRU_WIDTH), lambda b, i: (tile_of(b, i), 0)),
                pl.BlockSpec((SUBLANES, LRU_WIDTH), lambda b, i: (b * sblk + jnp.maximum(local(b, i) * hb - 1, 0), 0)),
                pl.BlockSpec((SUBLANES, LRU_WIDTH),
                             lambda b, i: (b * sblk + jnp.minimum((local(b, i) + 1) * hb, sblk - 1), 0))]

    wspec = [_const_spec((LRU_BLOCKS, LRU_BLOCK_DIM, LRU_BLOCK_DIM)), _const_spec((1, LRU_WIDTH)),
             _const_spec((LRU_BLOCKS, LRU_BLOCK_DIM, LRU_BLOCK_DIM)), _const_spec((1, LRU_WIDTH)),
             _const_spec((1, LRU_WIDTH))]
    out = jax.ShapeDtypeStruct((T_TOK, LRU_WIDTH), BF16)
    return pl.pallas_call(
        _lru_kernel,
        grid=(N_SEQ, SCAN_TILES),
        in_specs=specs(fidx) + specs(ridx) + [_const_spec(conv_w.shape), _const_spec((1, LRU_WIDTH))] + wspec + wspec,
        out_specs=[pl.BlockSpec((TS, LRU_WIDTH), lambda b, i: (fidx(b, i), 0)),
                   pl.BlockSpec((TS, LRU_WIDTH), lambda b, i: (ridx(b, i), 0))],
        out_shape=[out, out],
        scratch_shapes=[pltpu.VMEM((SUBLANES, LRU_WIDTH), F32), pltpu.VMEM((SUBLANES, LRU_WIDTH), F32)],
        compiler_params=_cparams("arbitrary", "arbitrary"),
        name="lru_scan",
    )(xb, xb, xb, xb, xb, xb, conv_w, conv_b, *fwd, *bwd)


def _outproj1_kernel(x_ref, hf_ref, hb_ref, gate_ref, w_ref, out_ref):
    y = (hf_ref[...].astype(F32) + hb_ref[...].astype(F32)) * gate_ref[...].astype(F32)
    out_ref[...] = x_ref[...] + jnp.dot(y.astype(BF16), w_ref[...], preferred_element_type=F32)


def _outproj1(x, hf, hb, gate, w):
    row = pl.BlockSpec((TM, D_MODEL), lambda i: (i, 0))
    return pl.pallas_call(
        _outproj1_kernel,
        grid=(T_TOK // TM,),
        in_specs=[row, row, row, row, _const_spec(w.shape)],
        out_specs=row,
        out_shape=jax.ShapeDtypeStruct((T_TOK, D_MODEL), F32),
        compiler_params=_cparams("arbitrary"),
        name="outproj1",
    )(x, hf, hb, gate, w)


def _rope_freq_lanes():
    inv_freq = ROPE_THETA ** (-jnp.arange(ROPE_HALF, dtype=F32) / ROPE_HALF)
    dim = jnp.arange(LANES) % HEAD_DIM
    return jnp.where(dim < ROPE_DIM, inv_freq[dim % ROPE_HALF], 0.0).astype(F32).reshape(1, LANES)


def _head_expand_matrix():
    src = jnp.arange(2 * LANES) % LANES
    dst = jnp.arange(ATTN_WIDTH) // HEAD_DIM
    return (src[:, None] == dst[None, :]).astype(BF16)


def kernel(x_prompt, x_sample, l0_ffn1_norm, l0_ffn1_w_gate, l0_ffn1_w_up, l0_ffn1_w_down, l0_mix_norm, l0_w_in, l0_conv_w, l0_w_out, l0_ffn2_norm, l0_ffn2_w_gate, l0_ffn2_w_up, l0_ffn2_w_down, l1_ffn1_norm, l1_ffn1_w_gate, l1_ffn1_w_up, l1_ffn1_w_down, l1_mix_norm, l1_w_in, l1_conv_w, l1_conv_b, l1_fwd_w_a, l1_fwd_b_a, l1_fwd_w_i, l1_fwd_b_i, l1_fwd_lambda, l1_bwd_w_a, l1_bwd_b_a, l1_bwd_w_i, l1_bwd_b_i, l1_bwd_lambda, l1_w_out, l1_ffn2_norm, l1_ffn2_w_gate, l1_ffn2_w_up, l1_ffn2_w_down, final_norm):
    vec = lambda t: t.reshape(1, -1).astype(F32)
    wt = lambda t: t.astype(BF16)
    x = jnp.concatenate([x_prompt.reshape(-1, D_MODEL), x_sample.reshape(-1, D_MODEL)], axis=0)
    fn = vec(final_norm)

    x = _ffn(x, vec(l0_ffn1_norm), wt(l0_ffn1_w_gate), wt(l0_ffn1_w_up), wt(l0_ffn1_w_down), fn, False)
    p, gb, q, k, v = _inproj0(x, vec(l0_mix_norm), wt(l0_w_in), _rope_freq_lanes())
    branches = [_attention_branch(q, k, v, dil) for dil in DILATIONS]
    x = _outproj0(x, p, gb, [o for o, _ in branches], [l for _, l in branches], l0_conv_w.astype(F32),
                  _head_expand_matrix(), wt(l0_w_out))
    x = _ffn(x, vec(l0_ffn2_norm), wt(l0_ffn2_w_gate), wt(l0_ffn2_w_up), wt(l0_ffn2_w_down), fn, False)

    x = _ffn(x, vec(l1_ffn1_norm), wt(l1_ffn1_w_gate), wt(l1_ffn1_w_up), wt(l1_ffn1_w_down), fn, False)
    xb, gate = _inproj1(x, vec(l1_mix_norm), wt(l1_w_in))
    fwd = (wt(l1_fwd_w_a), vec(l1_fwd_b_a), wt(l1_fwd_w_i), vec(l1_fwd_b_i), vec(l1_fwd_lambda))
    bwd = (wt(l1_bwd_w_a), vec(l1_bwd_b_a), wt(l1_bwd_w_i), vec(l1_bwd_b_i), vec(l1_bwd_lambda))
    hf, hb = _lru(xb, l1_conv_w.astype(F32), vec(l1_conv_b), fwd, bwd)
    x = _outproj1(x, hf, hb, gate, wt(l1_w_out))
    x = _ffn(x, vec(l1_ffn2_norm), wt(l1_ffn2_w_gate), wt(l1_ffn2_w_up), wt(l1_ffn2_w_down), fn, True)

    y_prompt = x[:SEQ].reshape(x_prompt.shape)
    y_sample = x[SEQ:].reshape(x_sample.shape)
    return (y_prompt, y_sample)
```

```python
import functools
import math

import jax
import jax.numpy as jnp
from jax import lax
from jax.experimental import pallas as pl
from jax.experimental.pallas import tpu as pltpu

F32 = jnp.float32
BF16 = jnp.bfloat16

D_MODEL = 1024
SEQ = 16384
N_SEQ = 3
T_TOK = N_SEQ * SEQ
D_FF = 2816
EPS = 1e-6
MASK_VALUE = -1e30
LOG2E = math.log2(math.e)

CONV_WIDTH = 512
ATTN_HEADS = 8
HEAD_DIM = 64
ATTN_WIDTH = ATTN_HEADS * HEAD_DIM
ROPE_DIM = HEAD_DIM // 4
ROPE_HALF = ROPE_DIM // 2
ROPE_THETA = 500000.0
DILATIONS = (1, 4, 16)
HALF_WINDOW = 64
LRU_WIDTH = 1024
LRU_BLOCKS = 4
LRU_BLOCK_DIM = LRU_WIDTH // LRU_BLOCKS
LRU_C = 8.0

LANES = 128
SUBLANES = 8
BF16_ROWS = 16
VMEM_LIMIT = 56 * 1024 * 1024

TM = 512
TILES_PER_SEQ = SEQ // TM
N_TILES = T_TOK // TM
FF_CHUNKS = ((0, 1024), (1024, 1024), (2048, 768))
ATTN_SLABS = ATTN_WIDTH // LANES
TQ = 512
QB = 128
KW = QB + 2 * HALF_WINDOW
TS = 256
SCAN_TILES = SEQ // TS
LRU_SLABS = LRU_WIDTH // LANES
CHUNK = TS // SUBLANES
CHUNK_PITCH = CHUNK + SUBLANES


def _cparams(*sem):
    return pltpu.CompilerParams(dimension_semantics=sem, vmem_limit_bytes=VMEM_LIMIT)


def _const_spec(shape):
    nd = len(shape)
    return pl.BlockSpec(shape, lambda *_: (0,) * nd, pipeline_mode=pl.Buffered(1))


def _rmsnorm(x, g):
    ms = jnp.mean(x * x, axis=-1, keepdims=True)
    return x * lax.rsqrt(ms + EPS) * g


def _sigmoid(x):
    return 1.0 / (1.0 + jnp.exp2(x * (-LOG2E)))


def _ffn_kernel(*refs, final, split):
    if split is None:
        x_ref, g_ref, wg_ref, wu_ref, wd_ref, fn_ref, o_ref = refs
        x = x_ref[...]
    else:
        xa_ref, xb_ref, g_ref, wg_ref, wu_ref, wd_ref, fn_ref, o_ref = refs
        x = jnp.where(pl.program_id(0) < split, xa_ref[...], xb_ref[...])
    h = _rmsnorm(x, g_ref[...]).astype(BF16)
    acc = None
    for c0, cw in FF_CHUNKS:
        g = jnp.dot(h, wg_ref[:, c0:c0 + cw], preferred_element_type=F32)
        u = jnp.dot(h, wu_ref[:, c0:c0 + cw], preferred_element_type=F32)
        a = (g * _sigmoid(g) * u).astype(BF16)
        y = jnp.dot(a, wd_ref[c0:c0 + cw, :], preferred_element_type=F32)
        acc = y if acc is None else acc + y
    out = x + 0.5 * acc
    if final:
        out = _rmsnorm(out, fn_ref[...])
    o_ref[...] = out


def _ffn(x_parts, g, wg, wu, wd, fn, final=False, tile0=0, n_tiles=N_TILES):
    row = pl.BlockSpec((TM, D_MODEL), lambda i: (i, 0))
    if len(x_parts) == 1:
        split = None
        x_specs = [pl.BlockSpec((TM, D_MODEL), lambda i: (i + tile0, 0))]
    else:
        assert tile0 == 0
        split = x_parts[0].shape[0] // TM
        x_specs = [pl.BlockSpec((TM, D_MODEL), lambda i: (jnp.minimum(i, split - 1), 0)),
                   pl.BlockSpec((TM, D_MODEL), lambda i: (jnp.maximum(i - split, 0), 0))]
    return pl.pallas_call(
        functools.partial(_ffn_kernel, final=final, split=split),
        grid=(n_tiles,),
        in_specs=x_specs + [_const_spec((1, D_MODEL)), _const_spec((D_MODEL, D_FF)), _const_spec((D_MODEL, D_FF)),
                            _const_spec((D_FF, D_MODEL)), _const_spec((1, D_MODEL))],
        out_specs=row,
        out_shape=jax.ShapeDtypeStruct((n_tiles * TM, D_MODEL), F32),
        compiler_params=_cparams("arbitrary"),
        name="ffn_final" if final else "ffn",
    )(*x_parts, g, wg, wu, wd, fn)


def _rope(x, cos, sin_lo, sin_hi):
    return (x * cos + pltpu.roll(x, ATTN_WIDTH - ROPE_HALF, axis=1) * sin_lo
            + pltpu.roll(x, ROPE_HALF, axis=1) * sin_hi)


def _inproj0_kernel(x_ref, g_ref, w_ref, freq_ref, p_ref, gb_ref,
                    q1_ref, k1_ref, v1_ref, q4_ref, k4_ref, v4_ref, q16_ref, k16_ref, v16_ref, slab_ref):
    h = _rmsnorm(x_ref[...], g_ref[...]).astype(BF16)
    c = CONV_WIDTH
    proj = lambda j: jnp.dot(h, w_ref[:, j * c:(j + 1) * c], preferred_element_type=F32)
    u, gb, gc, q, k, v = (proj(j) for j in range(6))
    p_ref[...] = (gc * u).astype(BF16)
    gb_ref[...] = gb.astype(BF16)

    tile = pl.program_id(0) % TILES_PER_SEQ
    pos = (tile * TM + lax.broadcasted_iota(jnp.int32, (TM, LANES), 0)).astype(F32)
    ang = pos * freq_ref[...]
    cos1, sin1 = jnp.cos(ang), jnp.sin(ang)
    lane = lax.broadcasted_iota(jnp.int32, (TM, LANES), 1)
    low = (lane & ROPE_HALF) == 0
    cos = jnp.concatenate([cos1] * ATTN_SLABS, axis=1)
    sin_lo = jnp.concatenate([jnp.where(low, -sin1, 0.0)] * ATTN_SLABS, axis=1)
    sin_hi = jnp.concatenate([jnp.where(low, 0.0, sin1)] * ATTN_SLABS, axis=1)
    q = _rope(q, cos, sin_lo, sin_hi) * (HEAD_DIM ** -0.5)
    k = _rope(k, cos, sin_lo, sin_hi)

    outs = ((q, q1_ref, q4_ref, q16_ref), (k, k1_ref, k4_ref, k16_ref), (v, v1_ref, v4_ref, v16_ref))
    for a, (val, nat_ref, d4_ref, d16_ref) in enumerate(outs):
        nat_ref[...] = val.astype(BF16)
        for s in range(ATTN_SLABS):
            slab_ref[a * ATTN_SLABS + s] = val[:, s * LANES:(s + 1) * LANES]
        for dil, ref in ((4, d4_ref), (16, d16_ref)):
            for r in range(dil):
                cls = [slab_ref[a * ATTN_SLABS + s, pl.ds(r, TM // dil, stride=dil), :] for s in range(ATTN_SLABS)]
                ref[:, r * ATTN_WIDTH:(r + 1) * ATTN_WIDTH] = jnp.concatenate(cls, axis=1).astype(BF16)


def _inproj0(x, g, w, freq):
    row = pl.BlockSpec((TM, D_MODEL), lambda i: (i, 0))
    half = pl.BlockSpec((TM, CONV_WIDTH), lambda i: (i, 0))
    nat = jax.ShapeDtypeStruct((T_TOK, CONV_WIDTH), BF16)
    lay_specs, lay_shapes = [], []
    for dil in DILATIONS:
        lay_specs += [pl.BlockSpec((TM // dil, dil * ATTN_WIDTH), lambda i: (i, 0))] * 3
        lay_shapes += [jax.ShapeDtypeStruct((T_TOK // dil, dil * ATTN_WIDTH), BF16)] * 3
    return pl.pallas_call(
        _inproj0_kernel,
        grid=(N_TILES,),
        in_specs=[row, _const_spec((1, D_MODEL)), _const_spec(w.shape), _const_spec((1, LANES))],
        out_specs=[half, half] + lay_specs,
        out_shape=[nat, nat] + lay_shapes,
        scratch_shapes=[pltpu.VMEM((3 * ATTN_SLABS, TM, LANES), F32)],
        compiler_params=_cparams("arbitrary"),
        name="inproj0",
    )(x, g, w, freq)


def _attn_kernel(q_ref, km_ref, kp_ref, kn_ref, vm_ref, vp_ref, vn_ref, o_ref, lse_ref, kw_ref, vw_ref, *, seq_len):
    hw = HALF_WINDOW
    kw_ref[0:hw, :] = kp_ref[...]
    kw_ref[hw:hw + TQ, :] = km_ref[...]
    kw_ref[hw + TQ:, :] = kn_ref[...]
    vw_ref[0:hw, :] = vp_ref[...]
    vw_ref[hw:hw + TQ, :] = vm_ref[...]
    vw_ref[hw + TQ:, :] = vn_ref[...]

    tile0 = pl.program_id(2) * TQ
    row = lax.broadcasted_iota(jnp.int32, (QB, KW), 0)
    col = lax.broadcasted_iota(jnp.int32, (QB, KW), 1)
    head_lane = lax.broadcasted_iota(jnp.int32, (QB, 2 * LANES), 1) // HEAD_DIM
    head_mask = [(head_lane == c).astype(F32).astype(BF16) for c in range(4)]
    stat_lane = lax.broadcasted_iota(jnp.int32, (QB, LANES), 1)
    halves = [slice(h * 2 * LANES, (h + 1) * 2 * LANES) for h in range(2)]

    def body(j, carry):
        q0 = tile0 + j * QB
        off = pl.multiple_of(j * QB, QB)
        lo = jnp.maximum(row, hw - q0)
        hi = jnp.minimum(row + 2 * hw, seq_len - 1 + hw - q0)
        valid = (col >= lo) & (col <= hi)
        qb = q_ref[pl.ds(off, QB), :]
        kwin = kw_ref[pl.ds(off, KW), :]
        vwin = vw_ref[pl.ds(off, KW), :]
        scores = []
        for head in range(ATTN_HEADS):
            half, c = divmod(head, 4)
            qh = qb[:, halves[half]] * head_mask[c]
            s = lax.dot_general(qh, kwin[:, halves[half]], (((1,), (1,)), ((), ())), preferred_element_type=F32)
            scores.append(jnp.where(valid, s, MASK_VALUE))
        probs, inv_den = [], []
        lse_all = jnp.zeros((QB, LANES), F32)
        for head in range(ATTN_HEADS):
            s = scores[head]
            m = jnp.max(s, axis=-1, keepdims=True)
            p = jnp.exp(s - m)
            den = jnp.sum(p, axis=-1, keepdims=True)
            probs.append(p.astype(BF16))
            inv_den.append(1.0 / den)
            lse_all = jnp.where(stat_lane == head, m + jnp.log(den), lse_all)
        for half in range(2):
            acc = jnp.zeros((QB, 2 * LANES), F32)
            for c in range(4):
                head = half * 4 + c
                pv = jnp.dot(probs[head], vwin[:, halves[half]], preferred_element_type=F32)
                acc = jnp.where(head_lane == c, pv * inv_den[head], acc)
            o_ref[pl.ds(off, QB), halves[half]] = acc.astype(BF16)
        lse_ref[pl.ds(off, QB), :] = lse_all
        return carry

    lax.fori_loop(0, TQ // QB, body, 0)


def _attention_branch(q, k, v, dil):
    seq_len = SEQ // dil
    rows = N_SEQ * seq_len
    tiles = seq_len // TQ
    hblk = seq_len // HALF_WINDOW
    main = pl.BlockSpec((TQ, ATTN_WIDTH), lambda b, r, j: (b * tiles + j, r))
    prev = pl.BlockSpec((HALF_WINDOW, ATTN_WIDTH),
                        lambda b, r, j: (b * hblk + jnp.maximum(j * (TQ // HALF_WINDOW) - 1, 0), r))
    nxt = pl.BlockSpec((HALF_WINDOW, ATTN_WIDTH),
                       lambda b, r, j: (b * hblk + jnp.minimum((j + 1) * (TQ // HALF_WINDOW), hblk - 1), r))
    stat = pl.BlockSpec((TQ, LANES), lambda b, r, j: (b * tiles + j, r))
    return pl.pallas_call(
        functools.partial(_attn_kernel, seq_len=seq_len),
        grid=(N_SEQ, dil, tiles),
        in_specs=[main, main, prev, nxt, main, prev, nxt],
        out_specs=[main, stat],
        out_shape=[jax.ShapeDtypeStruct((rows, dil * ATTN_WIDTH), BF16),
                   jax.ShapeDtypeStruct((rows, dil * LANES), F32)],
        scratch_shapes=[pltpu.VMEM((TQ + 2 * HALF_WINDOW, ATTN_WIDTH), BF16),
                        pltpu.VMEM((TQ + 2 * HALF_WINDOW, ATTN_WIDTH), BF16)],
        compiler_params=_cparams("arbitrary", "arbitrary", "arbitrary"),
        name=f"attn_d{dil}",
    )(q, k, k, k, v, v, v)


def _shifted(prev, cur, nxt, shift):
    ext = jnp.concatenate([prev, cur, nxt], axis=0)
    n = ext.shape[0]
    lo = prev.shape[0]
    return pltpu.roll(ext, shift % n, axis=0)[lo:lo + cur.shape[0]]


def _token_order(o_ref, l_ref, oslab_ref, lslab_ref, dil):
    rows = TM // dil
    for r in range(dil):
        lslab_ref[pl.ds(r, rows, stride=dil), :] = l_ref[:, r * LANES:(r + 1) * LANES]
        for s in range(ATTN_SLABS):
            c0 = r * ATTN_WIDTH + s * LANES
            oslab_ref[s, pl.ds(r, rows, stride=dil), :] = o_ref[:, c0:c0 + LANES].astype(F32)
    return jnp.concatenate([oslab_ref[s] for s in range(ATTN_SLABS)], axis=1), lslab_ref[...]


def _outproj0_kernel(x_ref, p_ref, pp_ref, pn_ref, gb_ref, o1_ref, o4_ref, o16_ref, l1_ref, l4_ref, l16_ref,
                     cw_ref, ex_ref, w_ref, out_ref, os4_ref, ls4_ref, os16_ref, ls16_ref):
    tile = pl.program_id(0) % TILES_PER_SEQ
    first = (tile == 0)
    last = (tile == TILES_PER_SEQ - 1)
    cur = p_ref[...].astype(F32)
    prev = jnp.where(first, 0.0, pp_ref[...].astype(F32))
    nxt = jnp.where(last, 0.0, pn_ref[...].astype(F32))
    cw = cw_ref[...]
    conv = (cw[0:1, :] * _shifted(prev, cur, nxt, 1) + cw[1:2, :] * cur + cw[2:3, :] * _shifted(prev, cur, nxt, -1))
    ya = gb_ref[...].astype(F32) * conv

    o1, l1 = o1_ref[...].astype(F32), l1_ref[...]
    o2, l2 = _token_order(o4_ref, l4_ref, os4_ref, ls4_ref, 4)
    o3, l3 = _token_order(o16_ref, l16_ref, os16_ref, ls16_ref, 16)
    m = jnp.maximum(jnp.maximum(l1, l2), l3)
    e1, e2, e3 = jnp.exp(l1 - m), jnp.exp(l2 - m), jnp.exp(l3 - m)
    tot = e1 + e2 + e3
    yb = None
    for e, o in ((e1, o1), (e2, o2), (e3, o3)):
        wgt = e / tot
        hi = wgt.astype(BF16)
        lo = (wgt - hi.astype(F32)).astype(BF16)
        wide = jnp.dot(jnp.concatenate([hi, lo], axis=1), ex_ref[...], preferred_element_type=F32)
        term = wide * o
        yb = term if yb is None else yb + term
    y = jnp.concatenate([ya.astype(BF16), yb.astype(BF16)], axis=1)
    out_ref[...] = x_ref[...] + jnp.dot(y, w_ref[...], preferred_element_type=F32)


def _outproj0(x, p, gb, outs, lses, conv_w, expand, w):
    row = pl.BlockSpec((TM, D_MODEL), lambda i: (i, 0))
    half = pl.BlockSpec((TM, CONV_WIDTH), lambda i: (i, 0))
    hb = TM // BF16_ROWS
    nblk = T_TOK // BF16_ROWS
    prev = pl.BlockSpec((BF16_ROWS, CONV_WIDTH), lambda i: (jnp.maximum(i * hb - 1, 0), 0))
    nxt = pl.BlockSpec((BF16_ROWS, CONV_WIDTH), lambda i: (jnp.minimum((i + 1) * hb, nblk - 1), 0))
    o_specs = [pl.BlockSpec((TM // dil, dil * ATTN_WIDTH), lambda i: (i, 0)) for dil in DILATIONS]
    l_specs = [pl.BlockSpec((TM // dil, dil * LANES), lambda i: (i, 0)) for dil in DILATIONS]
    return pl.pallas_call(
        _outproj0_kernel,
        grid=(N_TILES,),
        in_specs=[row, half, prev, nxt, half] + o_specs + l_specs
                 + [_const_spec(conv_w.shape), _const_spec(expand.shape), _const_spec(w.shape)],
        out_specs=row,
        out_shape=jax.ShapeDtypeStruct((T_TOK, D_MODEL), F32),
        scratch_shapes=[pltpu.VMEM((ATTN_SLABS, TM, LANES), F32), pltpu.VMEM((TM, LANES), F32),
                        pltpu.VMEM((ATTN_SLABS, TM, LANES), F32), pltpu.VMEM((TM, LANES), F32)],
        compiler_params=_cparams("arbitrary"),
        name="outproj0",
    )(x, p, p, p, gb, *outs, *lses, conv_w, expand, w)


def _inproj1_kernel(x_ref, g_ref, w_ref, xb_ref, gate_ref):
    h = _rmsnorm(x_ref[...], g_ref[...]).astype(BF16)
    z = jnp.dot(h, w_ref[...], preferred_element_type=F32)
    xb_ref[...] = z[:, :LRU_WIDTH]
    gate = z[:, LRU_WIDTH:]
    inner = math.sqrt(2.0 / math.pi) * (gate + 0.044715 * (gate * gate * gate))
    gate_ref[...] = (0.5 * gate * (1.0 + jnp.tanh(inner))).astype(BF16)


def _inproj1(x, g, w):
    row = pl.BlockSpec((TM, D_MODEL), lambda i: (i, 0))
    return pl.pallas_call(
        _inproj1_kernel,
        grid=(N_TILES,),
        in_specs=[row, _const_spec((1, D_MODEL)), _const_spec(w.shape)],
        out_specs=[row, row],
        out_shape=[jax.ShapeDtypeStruct((T_TOK, LRU_WIDTH), F32), jax.ShapeDtypeStruct((T_TOK, LRU_WIDTH), BF16)],
        compiler_params=_cparams("arbitrary"),
        name="inproj1",
    )(x, g, w)


def _lru_terms(xb, wa_ref, ba_ref, wi_ref, bi_ref, lam_ref):
    xg = xb.astype(BF16)
    ra, ia = [], []
    for g in range(LRU_BLOCKS):
        blk = xg[:, g * LRU_BLOCK_DIM:(g + 1) * LRU_BLOCK_DIM]
        ra.append(jnp.dot(blk, wa_ref[g], preferred_element_type=F32))
        ia.append(jnp.dot(blk, wi_ref[g], preferred_element_type=F32))
    r = _sigmoid(jnp.concatenate(ra, axis=1) + ba_ref[...])
    i = _sigmoid(jnp.concatenate(ia, axis=1) + bi_ref[...])
    z = -lam_ref[...]
    softplus = jnp.maximum(z, 0.0) + jnp.log(1.0 + jnp.exp(-jnp.abs(z)))
    a = jnp.exp2(r * ((-LRU_C * LOG2E) * softplus))
    b = jnp.sqrt(1.0 - a * a) * i * xb
    return a, b


def _chunk_rows(ref, j):
    return jnp.concatenate([ref[s, pl.ds(j, SUBLANES, stride=CHUNK_PITCH), :] for s in range(LRU_SLABS)], axis=1)


def _chunked_conv(cur_ref, prev_ref, next_ref, cw_ref, cb_ref, first, last, xs_ref):
    cur = cur_ref[...]
    prev = jnp.where(first, 0.0, prev_ref[...])
    nxt = jnp.where(last, 0.0, next_ref[...])
    for c in range(SUBLANES):
        follow = cur[(c + 1) * CHUNK:(c + 1) * CHUNK + SUBLANES] if c + 1 < SUBLANES else nxt
        for s in range(LRU_SLABS):
            lanes = slice(s * LANES, (s + 1) * LANES)
            xs_ref[s, c * CHUNK_PITCH:c * CHUNK_PITCH + CHUNK, :] = cur[c * CHUNK:(c + 1) * CHUNK, lanes]
            xs_ref[s, c * CHUNK_PITCH + CHUNK:(c + 1) * CHUNK_PITCH, :] = follow[:, lanes]
    x = {j: _chunk_rows(xs_ref, j) for j in range(CHUNK + 1)}
    sub = lax.broadcasted_iota(jnp.int32, (SUBLANES, LRU_WIDTH), 0)
    for back in (1, 2):
        x[-back] = jnp.where(sub == 0, prev[SUBLANES - back:SUBLANES - back + 1, :],
                             pltpu.roll(x[CHUNK - back], 1, axis=0))
    w = [jnp.broadcast_to(cw_ref[t:t + 1, :], (SUBLANES, LRU_WIDTH)) for t in range(4)]
    bias = jnp.broadcast_to(cb_ref[...], (SUBLANES, LRU_WIDTH))
    return jnp.concatenate(
        [w[0] * x[j - 2] + w[1] * x[j - 1] + w[2] * x[j] + w[3] * x[j + 1] + bias for j in range(CHUNK)], axis=0)


def _scan_tile(a, b, h_ref, carry, reverse):
    steps = range(CHUNK - 1, -1, -1) if reverse else range(CHUNK)
    local, prod = [None] * CHUNK, [None] * CHUNK
    h = p = None
    for j in steps:
        aj, bj = a[j * SUBLANES:(j + 1) * SUBLANES], b[j * SUBLANES:(j + 1) * SUBLANES]
        h = bj if h is None else aj * h + bj
        p = aj if p is None else aj * p
        local[j], prod[j] = h, p

    chunk_in = [None] * SUBLANES
    c_state = carry
    for c in (range(SUBLANES - 1, -1, -1) if reverse else range(SUBLANES)):
        chunk_in[c] = c_state
        c_state = h[c:c + 1, :] + p[c:c + 1, :] * c_state
    chunk_in = jnp.concatenate(chunk_in, axis=0)

    for j in range(CHUNK):
        full = local[j] + prod[j] * chunk_in
        for s in range(LRU_SLABS):
            h_ref[s, pl.ds(j, SUBLANES, stride=CHUNK_PITCH), :] = full[:, s * LANES:(s + 1) * LANES]
    out = jnp.concatenate(
        [jnp.concatenate([h_ref[s, c * CHUNK_PITCH:c * CHUNK_PITCH + CHUNK, :] for c in range(SUBLANES)], axis=0)
         for s in range(LRU_SLABS)], axis=1)
    return out, c_state


def _lru_kernel(xf_ref, xfp_ref, xfn_ref, xr_ref, xrp_ref, xrn_ref, cw_ref, cb_ref,
                fwa_ref, fba_ref, fwi_ref, fbi_ref, flam_ref, bwa_ref, bba_ref, bwi_ref, bbi_ref, blam_ref,
                hf_ref, hb_ref, cf_ref, cr_ref, xs_ref, h_ref):
    i = pl.program_id(1)

    @pl.when(i == 0)
    def _():
        cf_ref[...] = jnp.zeros_like(cf_ref)
        cr_ref[...] = jnp.zeros_like(cr_ref)

    def conv(cur_ref, prev_ref, next_ref, tile):
        return _chunked_conv(cur_ref, prev_ref, next_ref, cw_ref, cb_ref, tile == 0, tile == SCAN_TILES - 1, xs_ref)

    a, b = _lru_terms(conv(xf_ref, xfp_ref, xfn_ref, i), fwa_ref, fba_ref, fwi_ref, fbi_ref, flam_ref)
    h, carry = _scan_tile(a, b, h_ref, cf_ref[0:1, :], reverse=False)
    hf_ref[...] = h.astype(BF16)
    cf_ref[0:1, :] = carry

    a, b = _lru_terms(conv(xr_ref, xrp_ref, xrn_ref, SCAN_TILES - 1 - i), bwa_ref, bba_ref, bwi_ref, bbi_ref, blam_ref)
    h, carry = _scan_tile(a, b, h_ref, cr_ref[0:1, :], reverse=True)
    hb_ref[...] = h.astype(BF16)
    cr_ref[0:1, :] = carry


def _lru(xb, conv_w, conv_b, fwd, bwd):
    hb = TS // SUBLANES
    sblk = SEQ // SUBLANES
    fidx = lambda b, i: b * SCAN_TILES + i
    ridx = lambda b, i: b * SCAN_TILES + SCAN_TILES - 1 - i

    def specs(tile_of):
        local = lambda b, i: tile_of(b, i) - b * SCAN_TILES
        return [pl.BlockSpec((TS, LRU_WIDTH), lambda b, i: (tile_of(b, i), 0)),
                pl.BlockSpec((SUBLANES, LRU_WIDTH), lambda b, i: (b * sblk + jnp.maximum(local(b, i) * hb - 1, 0), 0)),
                pl.BlockSpec((SUBLANES, LRU_WIDTH),
                             lambda b, i: (b * sblk + jnp.minimum((local(b, i) + 1) * hb, sblk - 1), 0))]

    wspec = [_const_spec((LRU_BLOCKS, LRU_BLOCK_DIM, LRU_BLOCK_DIM)), _const_spec((1, LRU_WIDTH)),
             _const_spec((LRU_BLOCKS, LRU_BLOCK_DIM, LRU_BLOCK_DIM)), _const_spec((1, LRU_WIDTH)),
             _const_spec((1, LRU_WIDTH))]
    out = jax.ShapeDtypeStruct((T_TOK, LRU_WIDTH), BF16)
    scan_scratch = pltpu.VMEM((LRU_SLABS, SUBLANES * CHUNK_PITCH, LANES), F32)
    return pl.pallas_call(
        _lru_kernel,
        grid=(N_SEQ, SCAN_TILES),
        in_specs=specs(fidx) + specs(ridx) + [_const_spec(conv_w.shape), _const_spec((1, LRU_WIDTH))] + wspec + wspec,
        out_specs=[pl.BlockSpec((TS, LRU_WIDTH), lambda b, i: (fidx(b, i), 0)),
                   pl.BlockSpec((TS, LRU_WIDTH), lambda b, i: (ridx(b, i), 0))],
        out_shape=[out, out],
        scratch_shapes=[pltpu.VMEM((SUBLANES, LRU_WIDTH), F32), pltpu.VMEM((SUBLANES, LRU_WIDTH), F32),
                        scan_scratch, scan_scratch],
        compiler_params=_cparams("arbitrary", "arbitrary"),
        name="lru_scan",
    )(xb, xb, xb, xb, xb, xb, conv_w, conv_b, *fwd, *bwd)


def _outproj1_kernel(x_ref, hf_ref, hb_ref, gate_ref, w_ref, out_ref):
    y = (hf_ref[...].astype(F32) + hb_ref[...].astype(F32)) * gate_ref[...].astype(F32)
    out_ref[...] = x_ref[...] + jnp.dot(y.astype(BF16), w_ref[...], preferred_element_type=F32)


def _outproj1(x, hf, hb, gate, w):
    row = pl.BlockSpec((TM, D_MODEL), lambda i: (i, 0))
    return pl.pallas_call(
        _outproj1_kernel,
        grid=(N_TILES,),
        in_specs=[row, row, row, row, _const_spec(w.shape)],
        out_specs=row,
        out_shape=jax.ShapeDtypeStruct((T_TOK, D_MODEL), F32),
        compiler_params=_cparams("arbitrary"),
        name="outproj1",
    )(x, hf, hb, gate, w)


def _rope_freq_lanes():
    inv_freq = ROPE_THETA ** (-jnp.arange(ROPE_HALF, dtype=F32) / ROPE_HALF)
    dim = jnp.arange(LANES) % HEAD_DIM
    return jnp.where(dim < ROPE_DIM, inv_freq[dim % ROPE_HALF], 0.0).astype(F32).reshape(1, LANES)


def _head_expand_matrix():
    src = jnp.arange(2 * LANES) % LANES
    dst = jnp.arange(ATTN_WIDTH) // HEAD_DIM
    return (src[:, None] == dst[None, :]).astype(BF16)


def kernel(x_prompt, x_sample, l0_ffn1_norm, l0_ffn1_w_gate, l0_ffn1_w_up, l0_ffn1_w_down, l0_mix_norm, l0_w_in, l0_conv_w, l0_w_out, l0_ffn2_norm, l0_ffn2_w_gate, l0_ffn2_w_up, l0_ffn2_w_down, l1_ffn1_norm, l1_ffn1_w_gate, l1_ffn1_w_up, l1_ffn1_w_down, l1_mix_norm, l1_w_in, l1_conv_w, l1_conv_b, l1_fwd_w_a, l1_fwd_b_a, l1_fwd_w_i, l1_fwd_b_i, l1_fwd_lambda, l1_bwd_w_a, l1_bwd_b_a, l1_bwd_w_i, l1_bwd_b_i, l1_bwd_lambda, l1_w_out, l1_ffn2_norm, l1_ffn2_w_gate, l1_ffn2_w_up, l1_ffn2_w_down, final_norm):
    vec = lambda t: t.reshape(1, -1).astype(F32)
    wt = lambda t: t.astype(BF16)
    fn = vec(final_norm)
    x_parts = (x_prompt.reshape(-1, D_MODEL), x_sample.reshape(-1, D_MODEL))

    x = _ffn(x_parts, vec(l0_ffn1_norm), wt(l0_ffn1_w_gate), wt(l0_ffn1_w_up), wt(l0_ffn1_w_down), fn)
    p, gb, *qkv = _inproj0(x, vec(l0_mix_norm), wt(l0_w_in), _rope_freq_lanes())
    branches = [_attention_branch(*qkv[3 * n:3 * n + 3], dil) for n, dil in enumerate(DILATIONS)]
    x = _outproj0(x, p, gb, [o for o, _ in branches], [l for _, l in branches], l0_conv_w.astype(F32),
                  _head_expand_matrix(), wt(l0_w_out))
    x = _ffn((x,), vec(l0_ffn2_norm), wt(l0_ffn2_w_gate), wt(l0_ffn2_w_up), wt(l0_ffn2_w_down), fn)

    x = _ffn((x,), vec(l1_ffn1_norm), wt(l1_ffn1_w_gate), wt(l1_ffn1_w_up), wt(l1_ffn1_w_down), fn)
    xb, gate = _inproj1(x, vec(l1_mix_norm), wt(l1_w_in))
    fwd = (wt(l1_fwd_w_a), vec(l1_fwd_b_a), wt(l1_fwd_w_i), vec(l1_fwd_b_i), vec(l1_fwd_lambda))
    bwd = (wt(l1_bwd_w_a), vec(l1_bwd_b_a), wt(l1_bwd_w_i), vec(l1_bwd_b_i), vec(l1_bwd_lambda))
    hf, hb = _lru(xb, l1_conv_w.astype(F32), vec(l1_conv_b), fwd, bwd)
    x = _outproj1(x, hf, hb, gate, wt(l1_w_out))
    last = (vec(l1_ffn2_norm), wt(l1_ffn2_w_gate), wt(l1_ffn2_w_up), wt(l1_ffn2_w_down), fn)
    prompt_tiles = x_parts[0].shape[0] // TM
    y_prompt = _ffn((x,), *last, final=True, tile0=0, n_tiles=prompt_tiles)
    y_sample = _ffn((x,), *last, final=True, tile0=prompt_tiles, n_tiles=N_TILES - prompt_tiles)
    return (y_prompt.reshape(x_prompt.shape), y_sample.reshape(x_sample.shape))
```

```python
import functools
import math

import jax
import jax.numpy as jnp
from jax import lax
from jax.experimental import pallas as pl
from jax.experimental.pallas import tpu as pltpu

F32 = jnp.float32
BF16 = jnp.bfloat16

D_MODEL = 1024
SEQ = 16384
N_SEQ = 3
T_TOK = N_SEQ * SEQ
D_FF = 2816
EPS = 1e-6
MASK_VALUE = -1e30
LOG2E = math.log2(math.e)

CONV_WIDTH = 512
ATTN_HEADS = 8
HEAD_DIM = 64
ATTN_WIDTH = ATTN_HEADS * HEAD_DIM
ROPE_DIM = HEAD_DIM // 4
ROPE_HALF = ROPE_DIM // 2
ROPE_THETA = 500000.0
DILATIONS = (1, 4, 16)
HALF_WINDOW = 64
LRU_WIDTH = 1024
LRU_BLOCKS = 4
LRU_BLOCK_DIM = LRU_WIDTH // LRU_BLOCKS
LRU_C = 8.0

LANES = 128
SUBLANES = 8
BF16_ROWS = 16
VMEM_LIMIT = 56 * 1024 * 1024

TM = 512
TILES_PER_SEQ = SEQ // TM
N_TILES = T_TOK // TM
TF = 1024
FF_CHUNKS = ((0, 1024), (1024, 1024), (2048, 768))
ATTN_SLABS = ATTN_WIDTH // LANES
TQ = 512
QB = 128
KW = QB + 2 * HALF_WINDOW
TS = 512
SCAN_TILES = SEQ // TS
LRU_SLABS = LRU_WIDTH // LANES
CHUNK = TS // SUBLANES
CHUNK_PITCH = CHUNK + SUBLANES


def _cparams(*sem):
    return pltpu.CompilerParams(dimension_semantics=sem, vmem_limit_bytes=VMEM_LIMIT)


def _const_spec(shape):
    nd = len(shape)
    return pl.BlockSpec(shape, lambda *_: (0,) * nd, pipeline_mode=pl.Buffered(1))


def _rmsnorm(x, g):
    ms = jnp.mean(x * x, axis=-1, keepdims=True)
    return x * lax.rsqrt(ms + EPS) * g


def _sigmoid(x):
    return 1.0 / (1.0 + jnp.exp2(x * (-LOG2E)))


def _ffn_kernel(*refs, final, split):
    if split is None:
        x_ref, g_ref, wg_ref, wu_ref, wd_ref, fn_ref, o_ref = refs
        x = x_ref[...]
    else:
        xa_ref, xb_ref, g_ref, wg_ref, wu_ref, wd_ref, fn_ref, o_ref = refs
        x = jnp.where(pl.program_id(0) < split, xa_ref[...], xb_ref[...])
    h = _rmsnorm(x, g_ref[...]).astype(BF16)
    acc = None
    for c0, cw in FF_CHUNKS:
        g = jnp.dot(h, wg_ref[:, c0:c0 + cw], preferred_element_type=F32)
        u = jnp.dot(h, wu_ref[:, c0:c0 + cw], preferred_element_type=F32)
        a = (g * _sigmoid(g) * u).astype(BF16)
        y = jnp.dot(a, wd_ref[c0:c0 + cw, :], preferred_element_type=F32)
        acc = y if acc is None else acc + y
    out = x + 0.5 * acc
    if final:
        out = _rmsnorm(out, fn_ref[...])
    o_ref[...] = out


def _ffn(x_parts, g, wg, wu, wd, fn, final=False, tile0=0, n_tiles=T_TOK // TF):
    row = pl.BlockSpec((TF, D_MODEL), lambda i: (i, 0))
    if len(x_parts) == 1:
        split = None
        x_specs = [pl.BlockSpec((TF, D_MODEL), lambda i: (i + tile0, 0))]
    else:
        assert tile0 == 0
        split = x_parts[0].shape[0] // TF
        x_specs = [pl.BlockSpec((TF, D_MODEL), lambda i: (jnp.minimum(i, split - 1), 0)),
                   pl.BlockSpec((TF, D_MODEL), lambda i: (jnp.maximum(i - split, 0), 0))]
    return pl.pallas_call(
        functools.partial(_ffn_kernel, final=final, split=split),
        grid=(n_tiles,),
        in_specs=x_specs + [_const_spec((1, D_MODEL)), _const_spec((D_MODEL, D_FF)), _const_spec((D_MODEL, D_FF)),
                            _const_spec((D_FF, D_MODEL)), _const_spec((1, D_MODEL))],
        out_specs=row,
        out_shape=jax.ShapeDtypeStruct((n_tiles * TF, D_MODEL), F32),
        compiler_params=_cparams("arbitrary"),
        name="ffn_final" if final else "ffn",
    )(*x_parts, g, wg, wu, wd, fn)


def _rope(x, cos, sin_lo, sin_hi):
    return (x * cos + pltpu.roll(x, ATTN_WIDTH - ROPE_HALF, axis=1) * sin_lo
            + pltpu.roll(x, ROPE_HALF, axis=1) * sin_hi)


def _rope_table_kernel(freq_ref, cos_ref, sinlo_ref, sinhi_ref):
    pos = (pl.program_id(0) * TM + lax.broadcasted_iota(jnp.int32, (TM, LANES), 0)).astype(F32)
    ang = pos * freq_ref[...]
    sin = jnp.sin(ang)
    low = (lax.broadcasted_iota(jnp.int32, (TM, LANES), 1) & ROPE_HALF) == 0
    cos_ref[...] = jnp.cos(ang)
    sinlo_ref[...] = jnp.where(low, -sin, 0.0)
    sinhi_ref[...] = jnp.where(low, 0.0, sin)


def _rope_tables(freq):
    tab = pl.BlockSpec((TM, LANES), lambda i: (i, 0))
    shape = jax.ShapeDtypeStruct((SEQ, LANES), F32)
    return pl.pallas_call(
        _rope_table_kernel,
        grid=(TILES_PER_SEQ,),
        in_specs=[_const_spec((1, LANES))],
        out_specs=[tab, tab, tab],
        out_shape=[shape, shape, shape],
        compiler_params=_cparams("arbitrary"),
        name="rope_tables",
    )(freq)


def _inproj0_kernel(x_ref, g_ref, w_ref, cos_ref, sinlo_ref, sinhi_ref, p_ref, gb_ref,
                    q1_ref, k1_ref, v1_ref, q4_ref, k4_ref, v4_ref, q16_ref, k16_ref, v16_ref, slab_ref):
    h = _rmsnorm(x_ref[...], g_ref[...]).astype(BF16)
    c = CONV_WIDTH
    proj = lambda j: jnp.dot(h, w_ref[:, j * c:(j + 1) * c], preferred_element_type=F32)
    q, k, v = proj(3), proj(4), proj(5)
    u, gb, gc = proj(0), proj(1), proj(2)
    p_ref[...] = (gc * u).astype(BF16)
    gb_ref[...] = gb.astype(BF16)

    cos = jnp.concatenate([cos_ref[...]] * ATTN_SLABS, axis=1)
    sin_lo = jnp.concatenate([sinlo_ref[...]] * ATTN_SLABS, axis=1)
    sin_hi = jnp.concatenate([sinhi_ref[...]] * ATTN_SLABS, axis=1)
    q = _rope(q, cos, sin_lo, sin_hi) * (HEAD_DIM ** -0.5)
    k = _rope(k, cos, sin_lo, sin_hi)

    outs = ((q, q1_ref, q4_ref, q16_ref), (k, k1_ref, k4_ref, k16_ref), (v, v1_ref, v4_ref, v16_ref))
    for a, (val, nat_ref, d4_ref, d16_ref) in enumerate(outs):
        nat_ref[...] = val.astype(BF16)
        for s in range(ATTN_SLABS):
            slab_ref[a * ATTN_SLABS + s] = val[:, s * LANES:(s + 1) * LANES]
        for dil, ref in ((4, d4_ref), (16, d16_ref)):
            for r in range(dil):
                cls = [slab_ref[a * ATTN_SLABS + s, pl.ds(r, TM // dil, stride=dil), :] for s in range(ATTN_SLABS)]
                ref[:, r * ATTN_WIDTH:(r + 1) * ATTN_WIDTH] = jnp.concatenate(cls, axis=1).astype(BF16)


def _inproj0(x, g, w, tables):
    row = pl.BlockSpec((TM, D_MODEL), lambda i: (i, 0))
    tab = pl.BlockSpec((TM, LANES), lambda i: (i % TILES_PER_SEQ, 0))
    half = pl.BlockSpec((TM, CONV_WIDTH), lambda i: (i, 0))
    nat = jax.ShapeDtypeStruct((T_TOK, CONV_WIDTH), BF16)
    lay_specs, lay_shapes = [], []
    for dil in DILATIONS:
        lay_specs += [pl.BlockSpec((TM // dil, dil * ATTN_WIDTH), lambda i: (i, 0))] * 3
        lay_shapes += [jax.ShapeDtypeStruct((T_TOK // dil, dil * ATTN_WIDTH), BF16)] * 3
    return pl.pallas_call(
        _inproj0_kernel,
        grid=(N_TILES,),
        in_specs=[row, _const_spec((1, D_MODEL)), _const_spec(w.shape), tab, tab, tab],
        out_specs=[half, half] + lay_specs,
        out_shape=[nat, nat] + lay_shapes,
        scratch_shapes=[pltpu.VMEM((3 * ATTN_SLABS, TM, LANES), F32)],
        compiler_params=_cparams("arbitrary"),
        name="inproj0",
    )(x, g, w, *tables)


def _attn_kernel(q_ref, km_ref, kp_ref, kn_ref, vm_ref, vp_ref, vn_ref, o_ref, lse_ref, kw_ref, vw_ref, *, seq_len):
    hw = HALF_WINDOW
    kw_ref[0:hw, :] = kp_ref[...]
    kw_ref[hw:hw + TQ, :] = km_ref[...]
    kw_ref[hw + TQ:, :] = kn_ref[...]
    vw_ref[0:hw, :] = vp_ref[...]
    vw_ref[hw:hw + TQ, :] = vm_ref[...]
    vw_ref[hw + TQ:, :] = vn_ref[...]

    tile0 = pl.program_id(2) * TQ
    row = lax.broadcasted_iota(jnp.int32, (QB, KW), 0)
    col = lax.broadcasted_iota(jnp.int32, (QB, KW), 1)
    head_lane = lax.broadcasted_iota(jnp.int32, (QB, 2 * LANES), 1) // HEAD_DIM
    head_mask = [(head_lane == c).astype(F32).astype(BF16) for c in range(4)]
    stat_lane = lax.broadcasted_iota(jnp.int32, (QB, LANES), 1)
    halves = [slice(h * 2 * LANES, (h + 1) * 2 * LANES) for h in range(2)]

    def body(j, carry):
        q0 = tile0 + j * QB
        off = pl.multiple_of(j * QB, QB)
        lo = jnp.maximum(row, hw - q0)
        hi = jnp.minimum(row + 2 * hw, seq_len - 1 + hw - q0)
        valid = (col >= lo) & (col <= hi)
        qb = q_ref[pl.ds(off, QB), :]
        kwin = kw_ref[pl.ds(off, KW), :]
        vwin = vw_ref[pl.ds(off, KW), :]
        scores = []
        for head in range(ATTN_HEADS):
            half, c = divmod(head, 4)
            qh = qb[:, halves[half]] * head_mask[c]
            s = lax.dot_general(qh, kwin[:, halves[half]], (((1,), (1,)), ((), ())), preferred_element_type=F32)
            scores.append(jnp.where(valid, s, MASK_VALUE))
        probs, inv_den = [], []
        lse_all = jnp.zeros((QB, LANES), F32)
        for head in range(ATTN_HEADS):
            s = scores[head]
            m = jnp.max(s, axis=-1, keepdims=True)
            p = jnp.exp(s - m)
            den = jnp.sum(p, axis=-1, keepdims=True)
            probs.append(p.astype(BF16))
            inv_den.append(1.0 / den)
            lse_all = jnp.where(stat_lane == head, m + jnp.log(den), lse_all)
        for half in range(2):
            acc = jnp.zeros((QB, 2 * LANES), F32)
            for c in range(4):
                head = half * 4 + c
                pv = jnp.dot(probs[head], vwin[:, halves[half]], preferred_element_type=F32)
                acc = jnp.where(head_lane == c, pv * inv_den[head], acc)
            o_ref[pl.ds(off, QB), halves[half]] = acc.astype(BF16)
        lse_ref[pl.ds(off, QB), :] = lse_all
        return carry

    lax.fori_loop(0, TQ // QB, body, 0, unroll=True)


def _attention_branch(q, k, v, dil):
    seq_len = SEQ // dil
    rows = N_SEQ * seq_len
    tiles = seq_len // TQ
    hblk = seq_len // HALF_WINDOW
    main = pl.BlockSpec((TQ, ATTN_WIDTH), lambda b, r, j: (b * tiles + j, r))
    prev = pl.BlockSpec((HALF_WINDOW, ATTN_WIDTH),
                        lambda b, r, j: (b * hblk + jnp.maximum(j * (TQ // HALF_WINDOW) - 1, 0), r))
    nxt = pl.BlockSpec((HALF_WINDOW, ATTN_WIDTH),
                       lambda b, r, j: (b * hblk + jnp.minimum((j + 1) * (TQ // HALF_WINDOW), hblk - 1), r))
    stat = pl.BlockSpec((TQ, LANES), lambda b, r, j: (b * tiles + j, r))
    return pl.pallas_call(
        functools.partial(_attn_kernel, seq_len=seq_len),
        grid=(N_SEQ, dil, tiles),
        in_specs=[main, main, prev, nxt, main, prev, nxt],
        out_specs=[main, stat],
        out_shape=[jax.ShapeDtypeStruct((rows, dil * ATTN_WIDTH), BF16),
                   jax.ShapeDtypeStruct((rows, dil * LANES), F32)],
        scratch_shapes=[pltpu.VMEM((TQ + 2 * HALF_WINDOW, ATTN_WIDTH), BF16),
                        pltpu.VMEM((TQ + 2 * HALF_WINDOW, ATTN_WIDTH), BF16)],
        compiler_params=_cparams("arbitrary", "arbitrary", "arbitrary"),
        name=f"attn_d{dil}",
    )(q, k, k, k, v, v, v)


def _shifted(prev, cur, nxt, shift):
    ext = jnp.concatenate([prev, cur, nxt], axis=0)
    n = ext.shape[0]
    lo = prev.shape[0]
    return pltpu.roll(ext, shift % n, axis=0)[lo:lo + cur.shape[0]]


def _token_order(o_ref, l_ref, oslab_ref, lslab_ref, dil):
    rows = TM // dil
    for r in range(dil):
        lslab_ref[pl.ds(r, rows, stride=dil), :] = l_ref[:, r * LANES:(r + 1) * LANES]
        for s in range(ATTN_SLABS):
            c0 = r * ATTN_WIDTH + s * LANES
            oslab_ref[s, pl.ds(r, rows, stride=dil), :] = o_ref[:, c0:c0 + LANES].astype(F32)
    return jnp.concatenate([oslab_ref[s] for s in range(ATTN_SLABS)], axis=1), lslab_ref[...]


def _outproj0_kernel(x_ref, p_ref, pp_ref, pn_ref, gb_ref, o1_ref, o4_ref, o16_ref, l1_ref, l4_ref, l16_ref,
                     cw_ref, ex_ref, w_ref, out_ref, os4_ref, ls4_ref, os16_ref, ls16_ref):
    tile = pl.program_id(0) % TILES_PER_SEQ
    first = (tile == 0)
    last = (tile == TILES_PER_SEQ - 1)
    cur = p_ref[...].astype(F32)
    prev = jnp.where(first, 0.0, pp_ref[...].astype(F32))
    nxt = jnp.where(last, 0.0, pn_ref[...].astype(F32))
    cw = cw_ref[...]
    conv = (cw[0:1, :] * _shifted(prev, cur, nxt, 1) + cw[1:2, :] * cur + cw[2:3, :] * _shifted(prev, cur, nxt, -1))
    ya = gb_ref[...].astype(F32) * conv

    o1, l1 = o1_ref[...].astype(F32), l1_ref[...]
    o2, l2 = _token_order(o4_ref, l4_ref, os4_ref, ls4_ref, 4)
    o3, l3 = _token_order(o16_ref, l16_ref, os16_ref, ls16_ref, 16)
    m = jnp.maximum(jnp.maximum(l1, l2), l3)
    e1, e2, e3 = jnp.exp(l1 - m), jnp.exp(l2 - m), jnp.exp(l3 - m)
    tot = e1 + e2 + e3
    yb = None
    for e, o in ((e1, o1), (e2, o2), (e3, o3)):
        wgt = e / tot
        hi = wgt.astype(BF16)
        lo = (wgt - hi.astype(F32)).astype(BF16)
        wide = jnp.dot(jnp.concatenate([hi, lo], axis=1), ex_ref[...], preferred_element_type=F32)
        term = wide * o
        yb = term if yb is None else yb + term
    y = jnp.concatenate([ya.astype(BF16), yb.astype(BF16)], axis=1)
    out_ref[...] = x_ref[...] + jnp.dot(y, w_ref[...], preferred_element_type=F32)


def _outproj0(x, p, gb, outs, lses, conv_w, expand, w):
    row = pl.BlockSpec((TM, D_MODEL), lambda i: (i, 0))
    half = pl.BlockSpec((TM, CONV_WIDTH), lambda i: (i, 0))
    hb = TM // BF16_ROWS
    nblk = T_TOK // BF16_ROWS
    prev = pl.BlockSpec((BF16_ROWS, CONV_WIDTH), lambda i: (jnp.maximum(i * hb - 1, 0), 0))
    nxt = pl.BlockSpec((BF16_ROWS, CONV_WIDTH), lambda i: (jnp.minimum((i + 1) * hb, nblk - 1), 0))
    o_specs = [pl.BlockSpec((TM // dil, dil * ATTN_WIDTH), lambda i: (i, 0)) for dil in DILATIONS]
    l_specs = [pl.BlockSpec((TM // dil, dil * LANES), lambda i: (i, 0)) for dil in DILATIONS]
    return pl.pallas_call(
        _outproj0_kernel,
        grid=(N_TILES,),
        in_specs=[row, half, prev, nxt, half] + o_specs + l_specs
                 + [_const_spec(conv_w.shape), _const_spec(expand.shape), _const_spec(w.shape)],
        out_specs=row,
        out_shape=jax.ShapeDtypeStruct((T_TOK, D_MODEL), F32),
        scratch_shapes=[pltpu.VMEM((ATTN_SLABS, TM, LANES), F32), pltpu.VMEM((TM, LANES), F32),
                        pltpu.VMEM((ATTN_SLABS, TM, LANES), F32), pltpu.VMEM((TM, LANES), F32)],
        compiler_params=_cparams("arbitrary"),
        name="outproj0",
    )(x, p, p, p, gb, *outs, *lses, conv_w, expand, w)


def _inproj1_kernel(x_ref, g_ref, w_ref, xb_ref, gate_ref):
    h = _rmsnorm(x_ref[...], g_ref[...]).astype(BF16)
    z = jnp.dot(h, w_ref[...], preferred_element_type=F32)
    xb_ref[...] = z[:, :LRU_WIDTH]
    gate = z[:, LRU_WIDTH:]
    inner = math.sqrt(2.0 / math.pi) * (gate + 0.044715 * (gate * gate * gate))
    gate_ref[...] = (0.5 * gate * (1.0 + jnp.tanh(inner))).astype(BF16)


def _inproj1(x, g, w):
    row = pl.BlockSpec((TM, D_MODEL), lambda i: (i, 0))
    return pl.pallas_call(
        _inproj1_kernel,
        grid=(N_TILES,),
        in_specs=[row, _const_spec((1, D_MODEL)), _const_spec(w.shape)],
        out_specs=[row, row],
        out_shape=[jax.ShapeDtypeStruct((T_TOK, LRU_WIDTH), F32), jax.ShapeDtypeStruct((T_TOK, LRU_WIDTH), BF16)],
        compiler_params=_cparams("arbitrary"),
        name="inproj1",
    )(x, g, w)


def _lru_terms(xb, wa_ref, ba_ref, wi_ref, bi_ref, lam_ref):
    xg = xb.astype(BF16)
    ra, ia = [], []
    for g in range(LRU_BLOCKS):
        blk = xg[:, g * LRU_BLOCK_DIM:(g + 1) * LRU_BLOCK_DIM]
        ra.append(jnp.dot(blk, wa_ref[g], preferred_element_type=F32))
        ia.append(jnp.dot(blk, wi_ref[g], preferred_element_type=F32))
    r = _sigmoid(jnp.concatenate(ra, axis=1) + ba_ref[...])
    i = _sigmoid(jnp.concatenate(ia, axis=1) + bi_ref[...])
    z = -lam_ref[...]
    softplus = jnp.maximum(z, 0.0) + jnp.log(1.0 + jnp.exp(-jnp.abs(z)))
    a = jnp.exp2(r * ((-LRU_C * LOG2E) * softplus))
    t = 1.0 - a * a
    root = jnp.where(t == 0.0, 0.0, t * lax.rsqrt(t))
    b = root * i * xb
    return a, b


def _chunk_rows(ref, j):
    return jnp.concatenate([ref[s, pl.ds(j, SUBLANES, stride=CHUNK_PITCH), :] for s in range(LRU_SLABS)], axis=1)


def _chunked_conv(cur_ref, prev_ref, next_ref, cw_ref, cb_ref, first, last, xs_ref):
    cur = cur_ref[...]
    prev = jnp.where(first, 0.0, prev_ref[...])
    nxt = jnp.where(last, 0.0, next_ref[...])
    for c in range(SUBLANES):
        follow = cur[(c + 1) * CHUNK:(c + 1) * CHUNK + SUBLANES] if c + 1 < SUBLANES else nxt
        for s in range(LRU_SLABS):
            lanes = slice(s * LANES, (s + 1) * LANES)
            xs_ref[s, c * CHUNK_PITCH:c * CHUNK_PITCH + CHUNK, :] = cur[c * CHUNK:(c + 1) * CHUNK, lanes]
            xs_ref[s, c * CHUNK_PITCH + CHUNK:(c + 1) * CHUNK_PITCH, :] = follow[:, lanes]
    x = {j: _chunk_rows(xs_ref, j) for j in range(CHUNK + 1)}
    sub = lax.broadcasted_iota(jnp.int32, (SUBLANES, LRU_WIDTH), 0)
    for back in (1, 2):
        x[-back] = jnp.where(sub == 0, prev[SUBLANES - back:SUBLANES - back + 1, :],
                             pltpu.roll(x[CHUNK - back], 1, axis=0))
    w = [jnp.broadcast_to(cw_ref[t:t + 1, :], (SUBLANES, LRU_WIDTH)) for t in range(4)]
    bias = jnp.broadcast_to(cb_ref[...], (SUBLANES, LRU_WIDTH))
    return jnp.concatenate(
        [w[0] * x[j - 2] + w[1] * x[j - 1] + w[2] * x[j] + w[3] * x[j + 1] + bias for j in range(CHUNK)], axis=0)


def _scan_tile(a, b, h_ref, carry, reverse):
    steps = range(CHUNK - 1, -1, -1) if reverse else range(CHUNK)
    local, prod = [None] * CHUNK, [None] * CHUNK
    h = p = None
    for j in steps:
        aj, bj = a[j * SUBLANES:(j + 1) * SUBLANES], b[j * SUBLANES:(j + 1) * SUBLANES]
        h = bj if h is None else aj * h + bj
        p = aj if p is None else aj * p
        local[j], prod[j] = h, p

    chunk_in = [None] * SUBLANES
    c_state = carry
    for c in (range(SUBLANES - 1, -1, -1) if reverse else range(SUBLANES)):
        chunk_in[c] = c_state
        c_state = h[c:c + 1, :] + p[c:c + 1, :] * c_state
    chunk_in = jnp.concatenate(chunk_in, axis=0)

    for j in range(CHUNK):
        full = local[j] + prod[j] * chunk_in
        for s in range(LRU_SLABS):
            h_ref[s, pl.ds(j, SUBLANES, stride=CHUNK_PITCH), :] = full[:, s * LANES:(s + 1) * LANES]
    out = jnp.concatenate(
        [jnp.concatenate([h_ref[s, c * CHUNK_PITCH:c * CHUNK_PITCH + CHUNK, :] for c in range(SUBLANES)], axis=0)
         for s in range(LRU_SLABS)], axis=1)
    return out, c_state


def _lru_kernel(xf_ref, xfp_ref, xfn_ref, xr_ref, xrp_ref, xrn_ref, cw_ref, cb_ref,
                fwa_ref, fba_ref, fwi_ref, fbi_ref, flam_ref, bwa_ref, bba_ref, bwi_ref, bbi_ref, blam_ref,
                hf_ref, hb_ref, cf_ref, cr_ref, xs_ref, h_ref):
    i = pl.program_id(1)

    @pl.when(i == 0)
    def _():
        cf_ref[...] = jnp.zeros_like(cf_ref)
        cr_ref[...] = jnp.zeros_like(cr_ref)

    def conv(cur_ref, prev_ref, next_ref, tile):
        return _chunked_conv(cur_ref, prev_ref, next_ref, cw_ref, cb_ref, tile == 0, tile == SCAN_TILES - 1, xs_ref)

    a, b = _lru_terms(conv(xf_ref, xfp_ref, xfn_ref, i), fwa_ref, fba_ref, fwi_ref, fbi_ref, flam_ref)
    h, carry = _scan_tile(a, b, h_ref, cf_ref[0:1, :], reverse=False)
    hf_ref[...] = h.astype(BF16)
    cf_ref[0:1, :] = carry

    a, b = _lru_terms(conv(xr_ref, xrp_ref, xrn_ref, SCAN_TILES - 1 - i), bwa_ref, bba_ref, bwi_ref, bbi_ref, blam_ref)
    h, carry = _scan_tile(a, b, h_ref, cr_ref[0:1, :], reverse=True)
    hb_ref[...] = h.astype(BF16)
    cr_ref[0:1, :] = carry


def _lru(xb, conv_w, conv_b, fwd, bwd):
    hb = TS // SUBLANES
    sblk = SEQ // SUBLANES
    fidx = lambda b, i: b * SCAN_TILES + i
    ridx = lambda b, i: b * SCAN_TILES + SCAN_TILES - 1 - i

    def specs(tile_of):
        local = lambda b, i: tile_of(b, i) - b * SCAN_TILES
        return [pl.BlockSpec((TS, LRU_WIDTH), lambda b, i: (tile_of(b, i), 0)),
                pl.BlockSpec((SUBLANES, LRU_WIDTH), lambda b, i: (b * sblk + jnp.maximum(local(b, i) * hb - 1, 0), 0)),
                pl.BlockSpec((SUBLANES, LRU_WIDTH),
                             lambda b, i: (b * sblk + jnp.minimum((local(b, i) + 1) * hb, sblk - 1), 0))]

    wspec = [_const_spec((LRU_BLOCKS, LRU_BLOCK_DIM, LRU_BLOCK_DIM)), _const_spec((1, LRU_WIDTH)),
             _const_spec((LRU_BLOCKS, LRU_BLOCK_DIM, LRU_BLOCK_DIM)), _const_spec((1, LRU_WIDTH)),
             _const_spec((1, LRU_WIDTH))]
    out = jax.ShapeDtypeStruct((T_TOK, LRU_WIDTH), BF16)
    scan_scratch = pltpu.VMEM((LRU_SLABS, SUBLANES * CHUNK_PITCH, LANES), F32)
    return pl.pallas_call(
        _lru_kernel,
        grid=(N_SEQ, SCAN_TILES),
        in_specs=specs(fidx) + specs(ridx) + [_const_spec(conv_w.shape), _const_spec((1, LRU_WIDTH))] + wspec + wspec,
        out_specs=[pl.BlockSpec((TS, LRU_WIDTH), lambda b, i: (fidx(b, i), 0)),
                   pl.BlockSpec((TS, LRU_WIDTH), lambda b, i: (ridx(b, i), 0))],
        out_shape=[out, out],
        scratch_shapes=[pltpu.VMEM((SUBLANES, LRU_WIDTH), F32), pltpu.VMEM((SUBLANES, LRU_WIDTH), F32),
                        scan_scratch, scan_scratch],
        compiler_params=_cparams("arbitrary", "arbitrary"),
        name="lru_scan",
    )(xb, xb, xb, xb, xb, xb, conv_w, conv_b, *fwd, *bwd)


def _outproj1_kernel(x_ref, hf_ref, hb_ref, gate_ref, w_ref, out_ref):
    y = (hf_ref[...].astype(F32) + hb_ref[...].astype(F32)) * gate_ref[...].astype(F32)
    out_ref[...] = x_ref[...] + jnp.dot(y.astype(BF16), w_ref[...], preferred_element_type=F32)


def _outproj1(x, hf, hb, gate, w):
    row = pl.BlockSpec((TM, D_MODEL), lambda i: (i, 0))
    return pl.pallas_call(
        _outproj1_kernel,
        grid=(N_TILES,),
        in_specs=[row, row, row, row, _const_spec(w.shape)],
        out_specs=row,
        out_shape=jax.ShapeDtypeStruct((T_TOK, D_MODEL), F32),
        compiler_params=_cparams("arbitrary"),
        name="outproj1",
    )(x, hf, hb, gate, w)


def _rope_freq_lanes():
    inv_freq = ROPE_THETA ** (-jnp.arange(ROPE_HALF, dtype=F32) / ROPE_HALF)
    dim = jnp.arange(LANES) % HEAD_DIM
    return jnp.where(dim < ROPE_DIM, inv_freq[dim % ROPE_HALF], 0.0).astype(F32).reshape(1, LANES)


def _head_expand_matrix():
    src = jnp.arange(2 * LANES) % LANES
    dst = jnp.arange(ATTN_WIDTH) // HEAD_DIM
    return (src[:, None] == dst[None, :]).astype(BF16)


def kernel(x_prompt, x_sample, l0_ffn1_norm, l0_ffn1_w_gate, l0_ffn1_w_up, l0_ffn1_w_down, l0_mix_norm, l0_w_in, l0_conv_w, l0_w_out, l0_ffn2_norm, l0_ffn2_w_gate, l0_ffn2_w_up, l0_ffn2_w_down, l1_ffn1_norm, l1_ffn1_w_gate, l1_ffn1_w_up, l1_ffn1_w_down, l1_mix_norm, l1_w_in, l1_conv_w, l1_conv_b, l1_fwd_w_a, l1_fwd_b_a, l1_fwd_w_i, l1_fwd_b_i, l1_fwd_lambda, l1_bwd_w_a, l1_bwd_b_a, l1_bwd_w_i, l1_bwd_b_i, l1_bwd_lambda, l1_w_out, l1_ffn2_norm, l1_ffn2_w_gate, l1_ffn2_w_up, l1_ffn2_w_down, final_norm):
    vec = lambda t: t.reshape(1, -1).astype(F32)
    wt = lambda t: t.astype(BF16)
    fn = vec(final_norm)
    x_parts = (x_prompt.reshape(-1, D_MODEL), x_sample.reshape(-1, D_MODEL))

    x = _ffn(x_parts, vec(l0_ffn1_norm), wt(l0_ffn1_w_gate), wt(l0_ffn1_w_up), wt(l0_ffn1_w_down), fn)
    p, gb, *qkv = _inproj0(x, vec(l0_mix_norm), wt(l0_w_in), _rope_tables(_rope_freq_lanes()))
    branches = [_attention_branch(*qkv[3 * n:3 * n + 3], dil) for n, dil in enumerate(DILATIONS)]
    x = _outproj0(x, p, gb, [o for o, _ in branches], [l for _, l in branches], l0_conv_w.astype(F32),
                  _head_expand_matrix(), wt(l0_w_out))
    x = _ffn((x,), vec(l0_ffn2_norm), wt(l0_ffn2_w_gate), wt(l0_ffn2_w_up), wt(l0_ffn2_w_down), fn)

    x = _ffn((x,), vec(l1_ffn1_norm), wt(l1_ffn1_w_gate), wt(l1_ffn1_w_up), wt(l1_ffn1_w_down), fn)
    xb, gate = _inproj1(x, vec(l1_mix_norm), wt(l1_w_in))
    fwd = (wt(l1_fwd_w_a), vec(l1_fwd_b_a), wt(l1_fwd_w_i), vec(l1_fwd_b_i), vec(l1_fwd_lambda))
    bwd = (wt(l1_bwd_w_a), vec(l1_bwd_b_a), wt(l1_bwd_w_i), vec(l1_bwd_b_i), vec(l1_bwd_lambda))
    hf, hb = _lru(xb, l1_conv_w.astype(F32), vec(l1_conv_b), fwd, bwd)
    x = _outproj1(x, hf, hb, gate, wt(l1_w_out))
    last = (vec(l1_ffn2_norm), wt(l1_ffn2_w_gate), wt(l1_ffn2_w_up), wt(l1_ffn2_w_down), fn)
    prompt_tiles = x_parts[0].shape[0] // TF
    y_prompt = _ffn((x,), *last, final=True, tile0=0, n_tiles=prompt_tiles)
    y_sample = _ffn((x,), *last, final=True, tile0=prompt_tiles, n_tiles=T_TOK // TF - prompt_tiles)
    return (y_prompt.reshape(x_prompt.shape), y_sample.reshape(x_sample.shape))
```

```python
import functools
import math

import jax
import jax.numpy as jnp
from jax import lax
from jax.experimental import pallas as pl
from jax.experimental.pallas import tpu as pltpu

F32 = jnp.float32
BF16 = jnp.bfloat16

D_MODEL = 1024
SEQ = 16384
N_SEQ = 3
T_TOK = N_SEQ * SEQ
D_FF = 2816
EPS = 1e-6
MASK_VALUE = -1e30
LOG2E = math.log2(math.e)

CONV_WIDTH = 512
ATTN_HEADS = 8
HEAD_DIM = 64
ATTN_WIDTH = ATTN_HEADS * HEAD_DIM
ROPE_DIM = HEAD_DIM // 4
ROPE_HALF = ROPE_DIM // 2
ROPE_THETA = 500000.0
DILATIONS = (1, 4, 16)
HALF_WINDOW = 64
LRU_WIDTH = 1024
LRU_BLOCKS = 4
LRU_BLOCK_DIM = LRU_WIDTH // LRU_BLOCKS
LRU_C = 8.0

LANES = 128
SUBLANES = 8
BF16_ROWS = 16
VMEM_LIMIT = 56 * 1024 * 1024

TM = 512
TILES_PER_SEQ = SEQ // TM
N_TILES = T_TOK // TM
TF = 1024
FF_CHUNKS = ((0, 1024), (1024, 1024), (2048, 768))
ATTN_SLABS = ATTN_WIDTH // LANES
TQ = 1024
QB = 128
KW = QB + 2 * HALF_WINDOW
TS = 512
SCAN_TILES = SEQ // TS
LRU_SLABS = LRU_WIDTH // LANES
CHUNK = TS // SUBLANES
CHUNK_PITCH = CHUNK + SUBLANES


def _cparams(*sem):
    return pltpu.CompilerParams(dimension_semantics=sem, vmem_limit_bytes=VMEM_LIMIT)


def _const_spec(shape):
    nd = len(shape)
    return pl.BlockSpec(shape, lambda *_: (0,) * nd, pipeline_mode=pl.Buffered(1))


def _rmsnorm(x, g):
    ms = jnp.mean(x * x, axis=-1, keepdims=True)
    return x * lax.rsqrt(ms + EPS) * g


def _sigmoid(x):
    return 1.0 / (1.0 + jnp.exp2(x * (-LOG2E)))


def _ffn_math(x, g_ref, wg_ref, wu_ref, wd_ref):
    h = _rmsnorm(x, g_ref[...]).astype(BF16)
    acc = None
    for c0, cw in FF_CHUNKS:
        g = jnp.dot(h, wg_ref[:, c0:c0 + cw], preferred_element_type=F32)
        u = jnp.dot(h, wu_ref[:, c0:c0 + cw], preferred_element_type=F32)
        a = (g * _sigmoid(g) * u).astype(BF16)
        y = jnp.dot(a, wd_ref[c0:c0 + cw, :], preferred_element_type=F32)
        acc = y if acc is None else acc + y
    return x + 0.5 * acc


def _ffn_weight_specs():
    return [_const_spec((1, D_MODEL)), _const_spec((D_MODEL, D_FF)), _const_spec((D_MODEL, D_FF)),
            _const_spec((D_FF, D_MODEL))]


def _ffn_kernel(xa_ref, xb_ref, g_ref, wg_ref, wu_ref, wd_ref, o_ref, *, split):
    x = jnp.where(pl.program_id(0) < split, xa_ref[...], xb_ref[...])
    o_ref[...] = _ffn_math(x, g_ref, wg_ref, wu_ref, wd_ref)


def _ffn(x_parts, ffn_w):
    split = x_parts[0].shape[0] // TF
    x_specs = [pl.BlockSpec((TF, D_MODEL), lambda i: (jnp.minimum(i, split - 1), 0)),
               pl.BlockSpec((TF, D_MODEL), lambda i: (jnp.maximum(i - split, 0), 0))]
    return pl.pallas_call(
        functools.partial(_ffn_kernel, split=split),
        grid=(T_TOK // TF,),
        in_specs=x_specs + _ffn_weight_specs(),
        out_specs=pl.BlockSpec((TF, D_MODEL), lambda i: (i, 0)),
        out_shape=jax.ShapeDtypeStruct((T_TOK, D_MODEL), F32),
        compiler_params=_cparams("arbitrary"),
        name="ffn",
    )(*x_parts, *ffn_w)


def _rope(x, cos, sin_lo, sin_hi):
    return (x * cos + pltpu.roll(x, ATTN_WIDTH - ROPE_HALF, axis=1) * sin_lo
            + pltpu.roll(x, ROPE_HALF, axis=1) * sin_hi)


def _rope_table_kernel(freq_ref, cos_ref, sinlo_ref, sinhi_ref):
    pos = (pl.program_id(0) * TM + lax.broadcasted_iota(jnp.int32, (TM, LANES), 0)).astype(F32)
    ang = pos * freq_ref[...]
    sin = jnp.sin(ang)
    low = (lax.broadcasted_iota(jnp.int32, (TM, LANES), 1) & ROPE_HALF) == 0
    cos_ref[...] = jnp.cos(ang)
    sinlo_ref[...] = jnp.where(low, -sin, 0.0)
    sinhi_ref[...] = jnp.where(low, 0.0, sin)


def _rope_tables(freq):
    tab = pl.BlockSpec((TM, LANES), lambda i: (i, 0))
    shape = jax.ShapeDtypeStruct((SEQ, LANES), F32)
    return pl.pallas_call(
        _rope_table_kernel,
        grid=(TILES_PER_SEQ,),
        in_specs=[_const_spec((1, LANES))],
        out_specs=[tab, tab, tab],
        out_shape=[shape, shape, shape],
        compiler_params=_cparams("arbitrary"),
        name="rope_tables",
    )(freq)


def _inproj0_kernel(x_ref, g_ref, w_ref, cos_ref, sinlo_ref, sinhi_ref, p_ref, gb_ref,
                    q1_ref, k1_ref, v1_ref, q4_ref, k4_ref, v4_ref, q16_ref, k16_ref, v16_ref, slab_ref):
    h = _rmsnorm(x_ref[...], g_ref[...]).astype(BF16)
    c = CONV_WIDTH
    proj = lambda j: jnp.dot(h, w_ref[:, j * c:(j + 1) * c], preferred_element_type=F32)
    q, k, v = proj(3), proj(4), proj(5)
    u, gb, gc = proj(0), proj(1), proj(2)
    p_ref[...] = (gc * u).astype(BF16)
    gb_ref[...] = gb.astype(BF16)

    cos = jnp.concatenate([cos_ref[...]] * ATTN_SLABS, axis=1)
    sin_lo = jnp.concatenate([sinlo_ref[...]] * ATTN_SLABS, axis=1)
    sin_hi = jnp.concatenate([sinhi_ref[...]] * ATTN_SLABS, axis=1)
    q = _rope(q, cos, sin_lo, sin_hi) * (HEAD_DIM ** -0.5)
    k = _rope(k, cos, sin_lo, sin_hi)

    outs = ((q, q1_ref, q4_ref, q16_ref), (k, k1_ref, k4_ref, k16_ref), (v, v1_ref, v4_ref, v16_ref))
    for a, (val, nat_ref, d4_ref, d16_ref) in enumerate(outs):
        nat_ref[...] = val.astype(BF16)
        for s in range(ATTN_SLABS):
            slab_ref[a * ATTN_SLABS + s] = val[:, s * LANES:(s + 1) * LANES]
        for dil, ref in ((4, d4_ref), (16, d16_ref)):
            for r in range(dil):
                cls = [slab_ref[a * ATTN_SLABS + s, pl.ds(r, TM // dil, stride=dil), :] for s in range(ATTN_SLABS)]
                ref[:, r * ATTN_WIDTH:(r + 1) * ATTN_WIDTH] = jnp.concatenate(cls, axis=1).astype(BF16)


def _inproj0(x, g, w, tables):
    row = pl.BlockSpec((TM, D_MODEL), lambda i: (i, 0))
    tab = pl.BlockSpec((TM, LANES), lambda i: (i % TILES_PER_SEQ, 0))
    half = pl.BlockSpec((TM, CONV_WIDTH), lambda i: (i, 0))
    nat = jax.ShapeDtypeStruct((T_TOK, CONV_WIDTH), BF16)
    lay_specs, lay_shapes = [], []
    for dil in DILATIONS:
        lay_specs += [pl.BlockSpec((TM // dil, dil * ATTN_WIDTH), lambda i: (i, 0))] * 3
        lay_shapes += [jax.ShapeDtypeStruct((T_TOK // dil, dil * ATTN_WIDTH), BF16)] * 3
    return pl.pallas_call(
        _inproj0_kernel,
        grid=(N_TILES,),
        in_specs=[row, _const_spec((1, D_MODEL)), _const_spec(w.shape), tab, tab, tab],
        out_specs=[half, half] + lay_specs,
        out_shape=[nat, nat] + lay_shapes,
        scratch_shapes=[pltpu.VMEM((3 * ATTN_SLABS, TM, LANES), F32)],
        compiler_params=_cparams("arbitrary"),
        name="inproj0",
    )(x, g, w, *tables)


def _attn_kernel(q_ref, km_ref, kp_ref, kn_ref, vm_ref, vp_ref, vn_ref, o_ref, lse_ref, kw_ref, vw_ref, *, seq_len):
    hw = HALF_WINDOW
    kw_ref[0:hw, :] = kp_ref[...]
    kw_ref[hw:hw + TQ, :] = km_ref[...]
    kw_ref[hw + TQ:, :] = kn_ref[...]
    vw_ref[0:hw, :] = vp_ref[...]
    vw_ref[hw:hw + TQ, :] = vm_ref[...]
    vw_ref[hw + TQ:, :] = vn_ref[...]

    tile0 = pl.program_id(2) * TQ
    row = lax.broadcasted_iota(jnp.int32, (QB, KW), 0)
    col = lax.broadcasted_iota(jnp.int32, (QB, KW), 1)
    head_lane = lax.broadcasted_iota(jnp.int32, (QB, 2 * LANES), 1) // HEAD_DIM
    head_mask = [(head_lane == c).astype(F32).astype(BF16) for c in range(4)]
    stat_lane = lax.broadcasted_iota(jnp.int32, (QB, LANES), 1)
    halves = [slice(h * 2 * LANES, (h + 1) * 2 * LANES) for h in range(2)]

    def body(j, carry):
        q0 = tile0 + j * QB
        off = pl.multiple_of(j * QB, QB)
        lo = jnp.maximum(row, hw - q0)
        hi = jnp.minimum(row + 2 * hw, seq_len - 1 + hw - q0)
        valid = (col >= lo) & (col <= hi)
        qb = q_ref[pl.ds(off, QB), :]
        kwin = kw_ref[pl.ds(off, KW), :]
        vwin = vw_ref[pl.ds(off, KW), :]
        scores = []
        for head in range(ATTN_HEADS):
            half, c = divmod(head, 4)
            qh = qb[:, halves[half]] * head_mask[c]
            s = lax.dot_general(qh, kwin[:, halves[half]], (((1,), (1,)), ((), ())), preferred_element_type=F32)
            scores.append(jnp.where(valid, s, MASK_VALUE))
        probs, inv_den = [], []
        lse_all = jnp.zeros((QB, LANES), F32)
        for head in range(ATTN_HEADS):
            s = scores[head]
            m = jnp.max(s, axis=-1, keepdims=True)
            p = jnp.exp(s - m)
            den = jnp.sum(p, axis=-1, keepdims=True)
            probs.append(p.astype(BF16))
            inv_den.append(1.0 / den)
            lse_all = jnp.where(stat_lane == head, m + jnp.log(den), lse_all)
        for half in range(2):
            acc = jnp.zeros((QB, 2 * LANES), F32)
            for c in range(4):
                head = half * 4 + c
                pv = jnp.dot(probs[head], vwin[:, halves[half]], preferred_element_type=F32)
                acc = jnp.where(head_lane == c, pv * inv_den[head], acc)
            o_ref[pl.ds(off, QB), halves[half]] = acc.astype(BF16)
        lse_ref[pl.ds(off, QB), :] = lse_all
        return carry

    lax.fori_loop(0, TQ // QB, body, 0, unroll=True)


def _attention_branch(q, k, v, dil):
    seq_len = SEQ // dil
    rows = N_SEQ * seq_len
    tiles = seq_len // TQ
    hblk = seq_len // HALF_WINDOW
    main = pl.BlockSpec((TQ, ATTN_WIDTH), lambda b, r, j: (b * tiles + j, r))
    prev = pl.BlockSpec((HALF_WINDOW, ATTN_WIDTH),
                        lambda b, r, j: (b * hblk + jnp.maximum(j * (TQ // HALF_WINDOW) - 1, 0), r))
    nxt = pl.BlockSpec((HALF_WINDOW, ATTN_WIDTH),
                       lambda b, r, j: (b * hblk + jnp.minimum((j + 1) * (TQ // HALF_WINDOW), hblk - 1), r))
    stat = pl.BlockSpec((TQ, LANES), lambda b, r, j: (b * tiles + j, r))
    return pl.pallas_call(
        functools.partial(_attn_kernel, seq_len=seq_len),
        grid=(N_SEQ, dil, tiles),
        in_specs=[main, main, prev, nxt, main, prev, nxt],
        out_specs=[main, stat],
        out_shape=[jax.ShapeDtypeStruct((rows, dil * ATTN_WIDTH), BF16),
                   jax.ShapeDtypeStruct((rows, dil * LANES), F32)],
        scratch_shapes=[pltpu.VMEM((TQ + 2 * HALF_WINDOW, ATTN_WIDTH), BF16),
                        pltpu.VMEM((TQ + 2 * HALF_WINDOW, ATTN_WIDTH), BF16)],
        compiler_params=_cparams("arbitrary", "arbitrary", "arbitrary"),
        name=f"attn_d{dil}",
    )(q, k, k, k, v, v, v)


def _shifted(prev, cur, nxt, shift):
    ext = jnp.concatenate([prev, cur, nxt], axis=0)
    n = ext.shape[0]
    lo = prev.shape[0]
    return pltpu.roll(ext, shift % n, axis=0)[lo:lo + cur.shape[0]]


def _token_order(o_ref, l_ref, oslab_ref, lslab_ref, dil):
    rows = TM // dil
    for r in range(dil):
        lslab_ref[pl.ds(r, rows, stride=dil), :] = l_ref[:, r * LANES:(r + 1) * LANES]
        for s in range(ATTN_SLABS):
            c0 = r * ATTN_WIDTH + s * LANES
            oslab_ref[s, pl.ds(r, rows, stride=dil), :] = o_ref[:, c0:c0 + LANES].astype(F32)
    return jnp.concatenate([oslab_ref[s] for s in range(ATTN_SLABS)], axis=1), lslab_ref[...]


def _outproj0_ffn_kernel(x_ref, p_ref, pp_ref, pn_ref, gb_ref, o1_ref, o4_ref, o16_ref, l1_ref, l4_ref, l16_ref,
                         cw_ref, ex_ref, w_ref, g_ref, wg_ref, wu_ref, wd_ref, out_ref,
                         os4_ref, ls4_ref, os16_ref, ls16_ref):
    tile = pl.program_id(0) % TILES_PER_SEQ
    first = (tile == 0)
    last = (tile == TILES_PER_SEQ - 1)
    cur = p_ref[...].astype(F32)
    prev = jnp.where(first, 0.0, pp_ref[...].astype(F32))
    nxt = jnp.where(last, 0.0, pn_ref[...].astype(F32))
    cw = cw_ref[...]
    conv = (cw[0:1, :] * _shifted(prev, cur, nxt, 1) + cw[1:2, :] * cur + cw[2:3, :] * _shifted(prev, cur, nxt, -1))
    ya = gb_ref[...].astype(F32) * conv

    o1, l1 = o1_ref[...].astype(F32), l1_ref[...]
    o2, l2 = _token_order(o4_ref, l4_ref, os4_ref, ls4_ref, 4)
    o3, l3 = _token_order(o16_ref, l16_ref, os16_ref, ls16_ref, 16)
    m = jnp.maximum(jnp.maximum(l1, l2), l3)
    e1, e2, e3 = jnp.exp(l1 - m), jnp.exp(l2 - m), jnp.exp(l3 - m)
    tot = e1 + e2 + e3
    yb = None
    for e, o in ((e1, o1), (e2, o2), (e3, o3)):
        wgt = e / tot
        hi = wgt.astype(BF16)
        lo = (wgt - hi.astype(F32)).astype(BF16)
        wide = jnp.dot(jnp.concatenate([hi, lo], axis=1), ex_ref[...], preferred_element_type=F32)
        term = wide * o
        yb = term if yb is None else yb + term
    y = jnp.concatenate([ya.astype(BF16), yb.astype(BF16)], axis=1)
    x = x_ref[...] + jnp.dot(y, w_ref[...], preferred_element_type=F32)
    out_ref[...] = _ffn_math(x, g_ref, wg_ref, wu_ref, wd_ref)


def _outproj0_ffn(x, p, gb, outs, lses, conv_w, expand, w, ffn_w):
    row = pl.BlockSpec((TM, D_MODEL), lambda i: (i, 0))
    half = pl.BlockSpec((TM, CONV_WIDTH), lambda i: (i, 0))
    hb = TM // BF16_ROWS
    nblk = T_TOK // BF16_ROWS
    prev = pl.BlockSpec((BF16_ROWS, CONV_WIDTH), lambda i: (jnp.maximum(i * hb - 1, 0), 0))
    nxt = pl.BlockSpec((BF16_ROWS, CONV_WIDTH), lambda i: (jnp.minimum((i + 1) * hb, nblk - 1), 0))
    o_specs = [pl.BlockSpec((TM // dil, dil * ATTN_WIDTH), lambda i: (i, 0)) for dil in DILATIONS]
    l_specs = [pl.BlockSpec((TM // dil, dil * LANES), lambda i: (i, 0)) for dil in DILATIONS]
    return pl.pallas_call(
        _outproj0_ffn_kernel,
        grid=(N_TILES,),
        in_specs=[row, half, prev, nxt, half] + o_specs + l_specs
                 + [_const_spec(conv_w.shape), _const_spec(expand.shape), _const_spec(w.shape)] + _ffn_weight_specs(),
        out_specs=row,
        out_shape=jax.ShapeDtypeStruct((T_TOK, D_MODEL), F32),
        scratch_shapes=[pltpu.VMEM((ATTN_SLABS, TM, LANES), F32), pltpu.VMEM((TM, LANES), F32),
                        pltpu.VMEM((ATTN_SLABS, TM, LANES), F32), pltpu.VMEM((TM, LANES), F32)],
        compiler_params=_cparams("arbitrary"),
        name="outproj0_ffn",
    )(x, p, p, p, gb, *outs, *lses, conv_w, expand, w, *ffn_w)


def _ffn_inproj1_kernel(x_ref, g_ref, wg_ref, wu_ref, wd_ref, g2_ref, w_ref, x_out_ref, xb_ref, gate_ref):
    x = _ffn_math(x_ref[...], g_ref, wg_ref, wu_ref, wd_ref)
    x_out_ref[...] = x
    h = _rmsnorm(x, g2_ref[...]).astype(BF16)
    z = jnp.dot(h, w_ref[...], preferred_element_type=F32)
    xb_ref[...] = z[:, :LRU_WIDTH]
    gate = z[:, LRU_WIDTH:]
    inner = math.sqrt(2.0 / math.pi) * (gate + 0.044715 * (gate * gate * gate))
    gate_ref[...] = (0.5 * gate * (1.0 + jnp.tanh(inner))).astype(BF16)


def _ffn_inproj1(x, ffn_w, g2, w):
    row = pl.BlockSpec((TM, D_MODEL), lambda i: (i, 0))
    return pl.pallas_call(
        _ffn_inproj1_kernel,
        grid=(N_TILES,),
        in_specs=[row] + _ffn_weight_specs() + [_const_spec((1, D_MODEL)), _const_spec(w.shape)],
        out_specs=[row, row, row],
        out_shape=[jax.ShapeDtypeStruct((T_TOK, D_MODEL), F32), jax.ShapeDtypeStruct((T_TOK, LRU_WIDTH), F32),
                   jax.ShapeDtypeStruct((T_TOK, LRU_WIDTH), BF16)],
        compiler_params=_cparams("arbitrary"),
        name="ffn_inproj1",
    )(x, *ffn_w, g2, w)


def _lru_terms(xb, wa_ref, ba_ref, wi_ref, bi_ref, lam_ref):
    xg = xb.astype(BF16)
    ra, ia = [], []
    for g in range(LRU_BLOCKS):
        blk = xg[:, g * LRU_BLOCK_DIM:(g + 1) * LRU_BLOCK_DIM]
        ra.append(jnp.dot(blk, wa_ref[g], preferred_element_type=F32))
        ia.append(jnp.dot(blk, wi_ref[g], preferred_element_type=F32))
    r = _sigmoid(jnp.concatenate(ra, axis=1) + ba_ref[...])
    i = _sigmoid(jnp.concatenate(ia, axis=1) + bi_ref[...])
    z = -lam_ref[...]
    softplus = jnp.maximum(z, 0.0) + jnp.log(1.0 + jnp.exp(-jnp.abs(z)))
    a = jnp.exp2(r * ((-LRU_C * LOG2E) * softplus))
    t = 1.0 - a * a
    root = jnp.where(t == 0.0, 0.0, t * lax.rsqrt(t))
    b = root * i * xb
    return a, b


def _chunk_rows(ref, j):
    return jnp.concatenate([ref[s, pl.ds(j, SUBLANES, stride=CHUNK_PITCH), :] for s in range(LRU_SLABS)], axis=1)


def _chunked_conv(cur_ref, prev_ref, next_ref, cw_ref, cb_ref, first, last, xs_ref):
    cur = cur_ref[...]
    prev = jnp.where(first, 0.0, prev_ref[...])
    nxt = jnp.where(last, 0.0, next_ref[...])
    for c in range(SUBLANES):
        follow = cur[(c + 1) * CHUNK:(c + 1) * CHUNK + SUBLANES] if c + 1 < SUBLANES else nxt
        for s in range(LRU_SLABS):
            lanes = slice(s * LANES, (s + 1) * LANES)
            xs_ref[s, c * CHUNK_PITCH:c * CHUNK_PITCH + CHUNK, :] = cur[c * CHUNK:(c + 1) * CHUNK, lanes]
            xs_ref[s, c * CHUNK_PITCH + CHUNK:(c + 1) * CHUNK_PITCH, :] = follow[:, lanes]
    x = {j: _chunk_rows(xs_ref, j) for j in range(CHUNK + 1)}
    sub = lax.broadcasted_iota(jnp.int32, (SUBLANES, LRU_WIDTH), 0)
    for back in (1, 2):
        x[-back] = jnp.where(sub == 0, prev[SUBLANES - back:SUBLANES - back + 1, :],
                             pltpu.roll(x[CHUNK - back], 1, axis=0))
    w = [jnp.broadcast_to(cw_ref[t:t + 1, :], (SUBLANES, LRU_WIDTH)) for t in range(4)]
    bias = jnp.broadcast_to(cb_ref[...], (SUBLANES, LRU_WIDTH))
    return jnp.concatenate(
        [w[0] * x[j - 2] + w[1] * x[j - 1] + w[2] * x[j] + w[3] * x[j + 1] + bias for j in range(CHUNK)], axis=0)


def _scan_tile(a, b, h_ref, carry, reverse):
    steps = range(CHUNK - 1, -1, -1) if reverse else range(CHUNK)
    local, prod = [None] * CHUNK, [None] * CHUNK
    h = p = None
    for j in steps:
        aj, bj = a[j * SUBLANES:(j + 1) * SUBLANES], b[j * SUBLANES:(j + 1) * SUBLANES]
        h = bj if h is None else aj * h + bj
        p = aj if p is None else aj * p
        local[j], prod[j] = h, p

    chunk_in = [None] * SUBLANES
    c_state = carry
    for c in (range(SUBLANES - 1, -1, -1) if reverse else range(SUBLANES)):
        chunk_in[c] = c_state
        c_state = h[c:c + 1, :] + p[c:c + 1, :] * c_state
    chunk_in = jnp.concatenate(chunk_in, axis=0)

    for j in range(CHUNK):
        full = local[j] + prod[j] * chunk_in
        for s in range(LRU_SLABS):
            h_ref[s, pl.ds(j, SUBLANES, stride=CHUNK_PITCH), :] = full[:, s * LANES:(s + 1) * LANES]
    out = jnp.concatenate(
        [jnp.concatenate([h_ref[s, c * CHUNK_PITCH:c * CHUNK_PITCH + CHUNK, :] for c in range(SUBLANES)], axis=0)
         for s in range(LRU_SLABS)], axis=1)
    return out, c_state


def _lru_kernel(xf_ref, xfp_ref, xfn_ref, xr_ref, xrp_ref, xrn_ref, cw_ref, cb_ref,
                fwa_ref, fba_ref, fwi_ref, fbi_ref, flam_ref, bwa_ref, bba_ref, bwi_ref, bbi_ref, blam_ref,
                hf_ref, hb_ref, cf_ref, cr_ref, xs_ref, h_ref):
    i = pl.program_id(1)

    @pl.when(i == 0)
    def _():
        cf_ref[...] = jnp.zeros_like(cf_ref)
        cr_ref[...] = jnp.zeros_like(cr_ref)

    def conv(cur_ref, prev_ref, next_ref, tile):
        return _chunked_conv(cur_ref, prev_ref, next_ref, cw_ref, cb_ref, tile == 0, tile == SCAN_TILES - 1, xs_ref)

    a, b = _lru_terms(conv(xf_ref, xfp_ref, xfn_ref, i), fwa_ref, fba_ref, fwi_ref, fbi_ref, flam_ref)
    h, carry = _scan_tile(a, b, h_ref, cf_ref[0:1, :], reverse=False)
    hf_ref[...] = h.astype(BF16)
    cf_ref[0:1, :] = carry

    a, b = _lru_terms(conv(xr_ref, xrp_ref, xrn_ref, SCAN_TILES - 1 - i), bwa_ref, bba_ref, bwi_ref, bbi_ref, blam_ref)
    h, carry = _scan_tile(a, b, h_ref, cr_ref[0:1, :], reverse=True)
    hb_ref[...] = h.astype(BF16)
    cr_ref[0:1, :] = carry


def _lru(xb, conv_w, conv_b, fwd, bwd):
    hb = TS // SUBLANES
    sblk = SEQ // SUBLANES
    fidx = lambda b, i: b * SCAN_TILES + i
    ridx = lambda b, i: b * SCAN_TILES + SCAN_TILES - 1 - i

    def specs(tile_of):
        local = lambda b, i: tile_of(b, i) - b * SCAN_TILES
        return [pl.BlockSpec((TS, LRU_WIDTH), lambda b, i: (tile_of(b, i), 0)),
                pl.BlockSpec((SUBLANES, LRU_WIDTH), lambda b, i: (b * sblk + jnp.maximum(local(b, i) * hb - 1, 0), 0)),
                pl.BlockSpec((SUBLANES, LRU_WIDTH),
                             lambda b, i: (b * sblk + jnp.minimum((local(b, i) + 1) * hb, sblk - 1), 0))]

    wspec = [_const_spec((LRU_BLOCKS, LRU_BLOCK_DIM, LRU_BLOCK_DIM)), _const_spec((1, LRU_WIDTH)),
             _const_spec((LRU_BLOCKS, LRU_BLOCK_DIM, LRU_BLOCK_DIM)), _const_spec((1, LRU_WIDTH)),
             _const_spec((1, LRU_WIDTH))]
    out = jax.ShapeDtypeStruct((T_TOK, LRU_WIDTH), BF16)
    scan_scratch = pltpu.VMEM((LRU_SLABS, SUBLANES * CHUNK_PITCH, LANES), F32)
    return pl.pallas_call(
        _lru_kernel,
        grid=(N_SEQ, SCAN_TILES),
        in_specs=specs(fidx) + specs(ridx) + [_const_spec(conv_w.shape), _const_spec((1, LRU_WIDTH))] + wspec + wspec,
        out_specs=[pl.BlockSpec((TS, LRU_WIDTH), lambda b, i: (fidx(b, i), 0)),
                   pl.BlockSpec((TS, LRU_WIDTH), lambda b, i: (ridx(b, i), 0))],
        out_shape=[out, out],
        scratch_shapes=[pltpu.VMEM((SUBLANES, LRU_WIDTH), F32), pltpu.VMEM((SUBLANES, LRU_WIDTH), F32),
                        scan_scratch, scan_scratch],
        compiler_params=_cparams("arbitrary", "arbitrary"),
        name="lru_scan",
    )(xb, xb, xb, xb, xb, xb, conv_w, conv_b, *fwd, *bwd)


def _outproj1_ffn_kernel(x_ref, hf_ref, hb_ref, gate_ref, w_ref, g_ref, wg_ref, wu_ref, wd_ref, fn_ref, out_ref):
    y = (hf_ref[...].astype(F32) + hb_ref[...].astype(F32)) * gate_ref[...].astype(F32)
    x = x_ref[...] + jnp.dot(y.astype(BF16), w_ref[...], preferred_element_type=F32)
    out_ref[...] = _rmsnorm(_ffn_math(x, g_ref, wg_ref, wu_ref, wd_ref), fn_ref[...])


def _outproj1_ffn(x, hf, hb, gate, w, ffn_w, fn, tile0, n_tiles):
    row = pl.BlockSpec((TM, D_MODEL), lambda i: (i + tile0, 0))
    return pl.pallas_call(
        _outproj1_ffn_kernel,
        grid=(n_tiles,),
        in_specs=[row, row, row, row, _const_spec(w.shape)] + _ffn_weight_specs() + [_const_spec((1, D_MODEL))],
        out_specs=pl.BlockSpec((TM, D_MODEL), lambda i: (i, 0)),
        out_shape=jax.ShapeDtypeStruct((n_tiles * TM, D_MODEL), F32),
        compiler_params=_cparams("arbitrary"),
        name="outproj1_ffn",
    )(x, hf, hb, gate, w, *ffn_w, fn)


def _rope_freq_lanes():
    inv_freq = ROPE_THETA ** (-jnp.arange(ROPE_HALF, dtype=F32) / ROPE_HALF)
    dim = jnp.arange(LANES) % HEAD_DIM
    return jnp.where(dim < ROPE_DIM, inv_freq[dim % ROPE_HALF], 0.0).astype(F32).reshape(1, LANES)


def _head_expand_matrix():
    src = jnp.arange(2 * LANES) % LANES
    dst = jnp.arange(ATTN_WIDTH) // HEAD_DIM
    return (src[:, None] == dst[None, :]).astype(BF16)


def kernel(x_prompt, x_sample, l0_ffn1_norm, l0_ffn1_w_gate, l0_ffn1_w_up, l0_ffn1_w_down, l0_mix_norm, l0_w_in, l0_conv_w, l0_w_out, l0_ffn2_norm, l0_ffn2_w_gate, l0_ffn2_w_up, l0_ffn2_w_down, l1_ffn1_norm, l1_ffn1_w_gate, l1_ffn1_w_up, l1_ffn1_w_down, l1_mix_norm, l1_w_in, l1_conv_w, l1_conv_b, l1_fwd_w_a, l1_fwd_b_a, l1_fwd_w_i, l1_fwd_b_i, l1_fwd_lambda, l1_bwd_w_a, l1_bwd_b_a, l1_bwd_w_i, l1_bwd_b_i, l1_bwd_lambda, l1_w_out, l1_ffn2_norm, l1_ffn2_w_gate, l1_ffn2_w_up, l1_ffn2_w_down, final_norm):
    vec = lambda t: t.reshape(1, -1).astype(F32)
    wt = lambda t: t.astype(BF16)
    ffn_w = lambda g, w_gate, w_up, w_down: (vec(g), wt(w_gate), wt(w_up), wt(w_down))
    x_parts = (x_prompt.reshape(-1, D_MODEL), x_sample.reshape(-1, D_MODEL))

    x = _ffn(x_parts, ffn_w(l0_ffn1_norm, l0_ffn1_w_gate, l0_ffn1_w_up, l0_ffn1_w_down))
    p, gb, *qkv = _inproj0(x, vec(l0_mix_norm), wt(l0_w_in), _rope_tables(_rope_freq_lanes()))
    branches = [_attention_branch(*qkv[3 * n:3 * n + 3], dil) for n, dil in enumerate(DILATIONS)]
    x = _outproj0_ffn(x, p, gb, [o for o, _ in branches], [l for _, l in branches], l0_conv_w.astype(F32),
                      _head_expand_matrix(), wt(l0_w_out),
                      ffn_w(l0_ffn2_norm, l0_ffn2_w_gate, l0_ffn2_w_up, l0_ffn2_w_down))

    x, xb, gate = _ffn_inproj1(x, ffn_w(l1_ffn1_norm, l1_ffn1_w_gate, l1_ffn1_w_up, l1_ffn1_w_down),
                               vec(l1_mix_norm), wt(l1_w_in))
    fwd = (wt(l1_fwd_w_a), vec(l1_fwd_b_a), wt(l1_fwd_w_i), vec(l1_fwd_b_i), vec(l1_fwd_lambda))
    bwd = (wt(l1_bwd_w_a), vec(l1_bwd_b_a), wt(l1_bwd_w_i), vec(l1_bwd_b_i), vec(l1_bwd_lambda))
    hf, hb = _lru(xb, l1_conv_w.astype(F32), vec(l1_conv_b), fwd, bwd)
    last = (x, hf, hb, gate, wt(l1_w_out), ffn_w(l1_ffn2_norm, l1_ffn2_w_gate, l1_ffn2_w_up, l1_ffn2_w_down),
            vec(final_norm))
    prompt_tiles = x_parts[0].shape[0] // TM
    y_prompt = _outproj1_ffn(*last, tile0=0, n_tiles=prompt_tiles)
    y_sample = _outproj1_ffn(*last, tile0=prompt_tiles, n_tiles=N_TILES - prompt_tiles)
    return (y_prompt.reshape(x_prompt.shape), y_sample.reshape(x_sample.shape))
```

```python
import functools
import math

import jax
import jax.numpy as jnp
from jax import lax
from jax.experimental import pallas as pl
from jax.experimental.pallas import tpu as pltpu

F32 = jnp.float32
BF16 = jnp.bfloat16

D_MODEL = 1024
SEQ = 16384
N_SEQ = 3
T_TOK = N_SEQ * SEQ
D_FF = 2816
EPS = 1e-6
MASK_VALUE = -1e30
LOG2E = math.log2(math.e)

CONV_WIDTH = 512
ATTN_HEADS = 8
HEAD_DIM = 64
ATTN_WIDTH = ATTN_HEADS * HEAD_DIM
ROPE_DIM = HEAD_DIM // 4
ROPE_HALF = ROPE_DIM // 2
ROPE_THETA = 500000.0
DILATIONS = (1, 4, 16)
HALF_WINDOW = 64
LRU_WIDTH = 1024
LRU_BLOCKS = 4
LRU_BLOCK_DIM = LRU_WIDTH // LRU_BLOCKS
LRU_C = 8.0

LANES = 128
SUBLANES = 8
BF16_ROWS = 16
VMEM_LIMIT = 56 * 1024 * 1024

TM = 512
TILES_PER_SEQ = SEQ // TM
N_TILES = T_TOK // TM
TF = 1024
FF_CHUNKS = ((0, 1536), (1536, 1280))
ATTN_SLABS = ATTN_WIDTH // LANES
TQ = 1024
QB = 128
KW = QB + 2 * HALF_WINDOW
TS = 512
SCAN_TILES = SEQ // TS
LRU_SLABS = LRU_WIDTH // LANES
CHUNK = TS // SUBLANES
CHUNK_PITCH = CHUNK + SUBLANES


def _cparams(*sem):
    return pltpu.CompilerParams(dimension_semantics=sem, vmem_limit_bytes=VMEM_LIMIT)


def _const_spec(shape):
    nd = len(shape)
    return pl.BlockSpec(shape, lambda *_: (0,) * nd, pipeline_mode=pl.Buffered(1))


def _rmsnorm(x, g):
    ms = jnp.mean(x * x, axis=-1, keepdims=True)
    return x * lax.rsqrt(ms + EPS) * g


def _sigmoid(x):
    return 1.0 / (1.0 + jnp.exp2(x * (-LOG2E)))


def _ffn_math(x, g_ref, wg_ref, wu_ref, wd_ref):
    h = _rmsnorm(x, g_ref[...]).astype(BF16)
    acc = None
    for c0, cw in FF_CHUNKS:
        g = jnp.dot(h, wg_ref[:, c0:c0 + cw], preferred_element_type=F32)
        u = jnp.dot(h, wu_ref[:, c0:c0 + cw], preferred_element_type=F32)
        a = (g * _sigmoid(g) * u).astype(BF16)
        y = jnp.dot(a, wd_ref[c0:c0 + cw, :], preferred_element_type=F32)
        acc = y if acc is None else acc + y
    return x + 0.5 * acc


def _ffn_weight_specs():
    return [_const_spec((1, D_MODEL)), _const_spec((D_MODEL, D_FF)), _const_spec((D_MODEL, D_FF)),
            _const_spec((D_FF, D_MODEL))]


def _ffn_kernel(xa_ref, xb_ref, g_ref, wg_ref, wu_ref, wd_ref, o_ref, *, split):
    x = jnp.where(pl.program_id(0) < split, xa_ref[...], xb_ref[...])
    o_ref[...] = _ffn_math(x, g_ref, wg_ref, wu_ref, wd_ref)


def _ffn(x_parts, ffn_w):
    split = x_parts[0].shape[0] // TF
    x_specs = [pl.BlockSpec((TF, D_MODEL), lambda i: (jnp.minimum(i, split - 1), 0)),
               pl.BlockSpec((TF, D_MODEL), lambda i: (jnp.maximum(i - split, 0), 0))]
    return pl.pallas_call(
        functools.partial(_ffn_kernel, split=split),
        grid=(T_TOK // TF,),
        in_specs=x_specs + _ffn_weight_specs(),
        out_specs=pl.BlockSpec((TF, D_MODEL), lambda i: (i, 0)),
        out_shape=jax.ShapeDtypeStruct((T_TOK, D_MODEL), F32),
        compiler_params=_cparams("arbitrary"),
        name="ffn",
    )(*x_parts, *ffn_w)


def _rope(x, cos, sin_lo, sin_hi):
    return (x * cos + pltpu.roll(x, ATTN_WIDTH - ROPE_HALF, axis=1) * sin_lo
            + pltpu.roll(x, ROPE_HALF, axis=1) * sin_hi)


def _rope_table_kernel(freq_ref, cos_ref, sinlo_ref, sinhi_ref):
    pos = (pl.program_id(0) * TM + lax.broadcasted_iota(jnp.int32, (TM, LANES), 0)).astype(F32)
    ang = pos * freq_ref[...]
    sin = jnp.sin(ang)
    low = (lax.broadcasted_iota(jnp.int32, (TM, LANES), 1) & ROPE_HALF) == 0
    cos_ref[...] = jnp.cos(ang)
    sinlo_ref[...] = jnp.where(low, -sin, 0.0)
    sinhi_ref[...] = jnp.where(low, 0.0, sin)


def _rope_tables(freq):
    tab = pl.BlockSpec((TM, LANES), lambda i: (i, 0))
    shape = jax.ShapeDtypeStruct((SEQ, LANES), F32)
    return pl.pallas_call(
        _rope_table_kernel,
        grid=(TILES_PER_SEQ,),
        in_specs=[_const_spec((1, LANES))],
        out_specs=[tab, tab, tab],
        out_shape=[shape, shape, shape],
        compiler_params=_cparams("arbitrary"),
        name="rope_tables",
    )(freq)


def _inproj0_kernel(x_ref, g_ref, w_ref, cos_ref, sinlo_ref, sinhi_ref, p_ref, gb_ref,
                    q1_ref, k1_ref, v1_ref, q4_ref, k4_ref, v4_ref, q16_ref, k16_ref, v16_ref, slab_ref):
    h = _rmsnorm(x_ref[...], g_ref[...]).astype(BF16)
    c = CONV_WIDTH
    proj = lambda j: jnp.dot(h, w_ref[:, j * c:(j + 1) * c], preferred_element_type=F32)
    q, k, v = proj(3), proj(4), proj(5)
    u, gb, gc = proj(0), proj(1), proj(2)
    p_ref[...] = (gc * u).astype(BF16)
    gb_ref[...] = gb.astype(BF16)

    cos = jnp.concatenate([cos_ref[...]] * ATTN_SLABS, axis=1)
    sin_lo = jnp.concatenate([sinlo_ref[...]] * ATTN_SLABS, axis=1)
    sin_hi = jnp.concatenate([sinhi_ref[...]] * ATTN_SLABS, axis=1)
    q = _rope(q, cos, sin_lo, sin_hi) * (HEAD_DIM ** -0.5)
    k = _rope(k, cos, sin_lo, sin_hi)

    outs = ((q, q1_ref, q4_ref, q16_ref), (k, k1_ref, k4_ref, k16_ref), (v, v1_ref, v4_ref, v16_ref))
    for a, (val, nat_ref, d4_ref, d16_ref) in enumerate(outs):
        nat_ref[...] = val.astype(BF16)
        for s in range(ATTN_SLABS):
            slab_ref[a * ATTN_SLABS + s] = val[:, s * LANES:(s + 1) * LANES]
        for dil, ref in ((4, d4_ref), (16, d16_ref)):
            for r in range(dil):
                cls = [slab_ref[a * ATTN_SLABS + s, pl.ds(r, TM // dil, stride=dil), :] for s in range(ATTN_SLABS)]
                ref[:, r * ATTN_WIDTH:(r + 1) * ATTN_WIDTH] = jnp.concatenate(cls, axis=1).astype(BF16)


def _inproj0(x, g, w, tables):
    row = pl.BlockSpec((TM, D_MODEL), lambda i: (i, 0))
    tab = pl.BlockSpec((TM, LANES), lambda i: (i % TILES_PER_SEQ, 0))
    half = pl.BlockSpec((TM, CONV_WIDTH), lambda i: (i, 0))
    nat = jax.ShapeDtypeStruct((T_TOK, CONV_WIDTH), BF16)
    lay_specs, lay_shapes = [], []
    for dil in DILATIONS:
        lay_specs += [pl.BlockSpec((TM // dil, dil * ATTN_WIDTH), lambda i: (i, 0))] * 3
        lay_shapes += [jax.ShapeDtypeStruct((T_TOK // dil, dil * ATTN_WIDTH), BF16)] * 3
    return pl.pallas_call(
        _inproj0_kernel,
        grid=(N_TILES,),
        in_specs=[row, _const_spec((1, D_MODEL)), _const_spec(w.shape), tab, tab, tab],
        out_specs=[half, half] + lay_specs,
        out_shape=[nat, nat] + lay_shapes,
        scratch_shapes=[pltpu.VMEM((3 * ATTN_SLABS, TM, LANES), F32)],
        compiler_params=_cparams("arbitrary"),
        name="inproj0",
    )(x, g, w, *tables)


def _attn_kernel(q_ref, km_ref, kp_ref, kn_ref, vm_ref, vp_ref, vn_ref, o_ref, lse_ref, kw_ref, vw_ref, *, seq_len):
    hw = HALF_WINDOW
    kw_ref[0:hw, :] = kp_ref[...]
    kw_ref[hw:hw + TQ, :] = km_ref[...]
    kw_ref[hw + TQ:, :] = kn_ref[...]
    vw_ref[0:hw, :] = vp_ref[...]
    vw_ref[hw:hw + TQ, :] = vm_ref[...]
    vw_ref[hw + TQ:, :] = vn_ref[...]

    tile0 = pl.program_id(2) * TQ
    row = lax.broadcasted_iota(jnp.int32, (QB, KW), 0)
    col = lax.broadcasted_iota(jnp.int32, (QB, KW), 1)
    head_lane = lax.broadcasted_iota(jnp.int32, (QB, 2 * LANES), 1) // HEAD_DIM
    head_mask = [(head_lane == c).astype(F32).astype(BF16) for c in range(4)]
    stat_lane = lax.broadcasted_iota(jnp.int32, (QB, LANES), 1)
    halves = [slice(h * 2 * LANES, (h + 1) * 2 * LANES) for h in range(2)]

    def body(j, carry):
        q0 = tile0 + j * QB
        off = pl.multiple_of(j * QB, QB)
        lo = jnp.maximum(row, hw - q0)
        hi = jnp.minimum(row + 2 * hw, seq_len - 1 + hw - q0)
        valid = (col >= lo) & (col <= hi)
        qb = q_ref[pl.ds(off, QB), :]
        kwin = kw_ref[pl.ds(off, KW), :]
        vwin = vw_ref[pl.ds(off, KW), :]
        scores = []
        for half in range(2):
            qh = jnp.concatenate([qb[:, halves[half]] * head_mask[c] for c in range(4)], axis=0)
            s = lax.dot_general(qh, kwin[:, halves[half]], (((1,), (1,)), ((), ())), preferred_element_type=F32)
            scores += [jnp.where(valid, s[c * QB:(c + 1) * QB], MASK_VALUE) for c in range(4)]
        probs, inv_den = [], []
        lse_all = jnp.zeros((QB, LANES), F32)
        for head in range(ATTN_HEADS):
            s = scores[head]
            m = jnp.max(s, axis=-1, keepdims=True)
            p = jnp.exp(s - m)
            den = jnp.sum(p, axis=-1, keepdims=True)
            probs.append(p.astype(BF16))
            inv_den.append(1.0 / den)
            lse_all = jnp.where(stat_lane == head, m + jnp.log(den), lse_all)
        for half in range(2):
            pv = jnp.dot(jnp.concatenate(probs[half * 4:half * 4 + 4], axis=0), vwin[:, halves[half]],
                         preferred_element_type=F32)
            acc = jnp.zeros((QB, 2 * LANES), F32)
            for c in range(4):
                acc = jnp.where(head_lane == c, pv[c * QB:(c + 1) * QB] * inv_den[half * 4 + c], acc)
            o_ref[pl.ds(off, QB), halves[half]] = acc.astype(BF16)
        lse_ref[pl.ds(off, QB), :] = lse_all
        return carry

    lax.fori_loop(0, TQ // QB, body, 0, unroll=True)


def _attention_branch(q, k, v, dil):
    seq_len = SEQ // dil
    rows = N_SEQ * seq_len
    tiles = seq_len // TQ
    hblk = seq_len // HALF_WINDOW
    main = pl.BlockSpec((TQ, ATTN_WIDTH), lambda b, r, j: (b * tiles + j, r))
    prev = pl.BlockSpec((HALF_WINDOW, ATTN_WIDTH),
                        lambda b, r, j: (b * hblk + jnp.maximum(j * (TQ // HALF_WINDOW) - 1, 0), r))
    nxt = pl.BlockSpec((HALF_WINDOW, ATTN_WIDTH),
                       lambda b, r, j: (b * hblk + jnp.minimum((j + 1) * (TQ // HALF_WINDOW), hblk - 1), r))
    stat = pl.BlockSpec((TQ, LANES), lambda b, r, j: (b * tiles + j, r))
    return pl.pallas_call(
        functools.partial(_attn_kernel, seq_len=seq_len),
        grid=(N_SEQ, dil, tiles),
        in_specs=[main, main, prev, nxt, main, prev, nxt],
        out_specs=[main, stat],
        out_shape=[jax.ShapeDtypeStruct((rows, dil * ATTN_WIDTH), BF16),
                   jax.ShapeDtypeStruct((rows, dil * LANES), F32)],
        scratch_shapes=[pltpu.VMEM((TQ + 2 * HALF_WINDOW, ATTN_WIDTH), BF16),
                        pltpu.VMEM((TQ + 2 * HALF_WINDOW, ATTN_WIDTH), BF16)],
        compiler_params=_cparams("arbitrary", "arbitrary", "arbitrary"),
        name=f"attn_d{dil}",
    )(q, k, k, k, v, v, v)


def _shifted(prev, cur, nxt, shift):
    ext = jnp.concatenate([prev, cur, nxt], axis=0)
    n = ext.shape[0]
    lo = prev.shape[0]
    return pltpu.roll(ext, shift % n, axis=0)[lo:lo + cur.shape[0]]


def _token_order(o_ref, l_ref, oslab_ref, lslab_ref, dil):
    rows = TM // dil
    for r in range(dil):
        lslab_ref[pl.ds(r, rows, stride=dil), :] = l_ref[:, r * LANES:(r + 1) * LANES]
        for s in range(ATTN_SLABS):
            c0 = r * ATTN_WIDTH + s * LANES
            oslab_ref[s, pl.ds(r, rows, stride=dil), :] = o_ref[:, c0:c0 + LANES].astype(F32)
    return jnp.concatenate([oslab_ref[s] for s in range(ATTN_SLABS)], axis=1), lslab_ref[...]


def _outproj0_ffn_kernel(x_ref, p_ref, pp_ref, pn_ref, gb_ref, o1_ref, o4_ref, o16_ref, l1_ref, l4_ref, l16_ref,
                         cw_ref, ex_ref, w_ref, g_ref, wg_ref, wu_ref, wd_ref, out_ref,
                         os4_ref, ls4_ref, os16_ref, ls16_ref):
    tile = pl.program_id(0) % TILES_PER_SEQ
    first = (tile == 0)
    last = (tile == TILES_PER_SEQ - 1)
    cur = p_ref[...].astype(F32)
    prev = jnp.where(first, 0.0, pp_ref[...].astype(F32))
    nxt = jnp.where(last, 0.0, pn_ref[...].astype(F32))
    cw = cw_ref[...]
    conv = (cw[0:1, :] * _shifted(prev, cur, nxt, 1) + cw[1:2, :] * cur + cw[2:3, :] * _shifted(prev, cur, nxt, -1))
    ya = gb_ref[...].astype(F32) * conv

    o1, l1 = o1_ref[...].astype(F32), l1_ref[...]
    o2, l2 = _token_order(o4_ref, l4_ref, os4_ref, ls4_ref, 4)
    o3, l3 = _token_order(o16_ref, l16_ref, os16_ref, ls16_ref, 16)
    m = jnp.maximum(jnp.maximum(l1, l2), l3)
    e1, e2, e3 = jnp.exp(l1 - m), jnp.exp(l2 - m), jnp.exp(l3 - m)
    tot = e1 + e2 + e3
    yb = None
    for e, o in ((e1, o1), (e2, o2), (e3, o3)):
        wgt = e / tot
        hi = wgt.astype(BF16)
        lo = (wgt - hi.astype(F32)).astype(BF16)
        wide = jnp.dot(jnp.concatenate([hi, lo], axis=1), ex_ref[...], preferred_element_type=F32)
        term = wide * o
        yb = term if yb is None else yb + term
    y = jnp.concatenate([ya.astype(BF16), yb.astype(BF16)], axis=1)
    x = x_ref[...] + jnp.dot(y, w_ref[...], preferred_element_type=F32)
    out_ref[...] = _ffn_math(x, g_ref, wg_ref, wu_ref, wd_ref)


def _outproj0_ffn(x, p, gb, outs, lses, conv_w, expand, w, ffn_w):
    row = pl.BlockSpec((TM, D_MODEL), lambda i: (i, 0))
    half = pl.BlockSpec((TM, CONV_WIDTH), lambda i: (i, 0))
    hb = TM // BF16_ROWS
    nblk = T_TOK // BF16_ROWS
    prev = pl.BlockSpec((BF16_ROWS, CONV_WIDTH), lambda i: (jnp.maximum(i * hb - 1, 0), 0))
    nxt = pl.BlockSpec((BF16_ROWS, CONV_WIDTH), lambda i: (jnp.minimum((i + 1) * hb, nblk - 1), 0))
    o_specs = [pl.BlockSpec((TM // dil, dil * ATTN_WIDTH), lambda i: (i, 0)) for dil in DILATIONS]
    l_specs = [pl.BlockSpec((TM // dil, dil * LANES), lambda i: (i, 0)) for dil in DILATIONS]
    return pl.pallas_call(
        _outproj0_ffn_kernel,
        grid=(N_TILES,),
        in_specs=[row, half, prev, nxt, half] + o_specs + l_specs
                 + [_const_spec(conv_w.shape), _const_spec(expand.shape), _const_spec(w.shape)] + _ffn_weight_specs(),
        out_specs=row,
        out_shape=jax.ShapeDtypeStruct((T_TOK, D_MODEL), F32),
        scratch_shapes=[pltpu.VMEM((ATTN_SLABS, TM, LANES), F32), pltpu.VMEM((TM, LANES), F32),
                        pltpu.VMEM((ATTN_SLABS, TM, LANES), F32), pltpu.VMEM((TM, LANES), F32)],
        compiler_params=_cparams("arbitrary"),
        name="outproj0_ffn",
    )(x, p, p, p, gb, *outs, *lses, conv_w, expand, w, *ffn_w)


def _ffn_inproj1_kernel(x_ref, g_ref, wg_ref, wu_ref, wd_ref, g2_ref, w_ref, x_out_ref, xb_ref, gate_ref):
    x = _ffn_math(x_ref[...], g_ref, wg_ref, wu_ref, wd_ref)
    x_out_ref[...] = x
    h = _rmsnorm(x, g2_ref[...]).astype(BF16)
    z = jnp.dot(h, w_ref[...], preferred_element_type=F32)
    xb_ref[...] = z[:, :LRU_WIDTH]
    gate = z[:, LRU_WIDTH:]
    inner = math.sqrt(2.0 / math.pi) * (gate + 0.044715 * (gate * gate * gate))
    gate_ref[...] = (0.5 * gate * (1.0 + jnp.tanh(inner))).astype(BF16)


def _ffn_inproj1(x, ffn_w, g2, w):
    row = pl.BlockSpec((TM, D_MODEL), lambda i: (i, 0))
    return pl.pallas_call(
        _ffn_inproj1_kernel,
        grid=(N_TILES,),
        in_specs=[row] + _ffn_weight_specs() + [_const_spec((1, D_MODEL)), _const_spec(w.shape)],
        out_specs=[row, row, row],
        out_shape=[jax.ShapeDtypeStruct((T_TOK, D_MODEL), F32), jax.ShapeDtypeStruct((T_TOK, LRU_WIDTH), F32),
                   jax.ShapeDtypeStruct((T_TOK, LRU_WIDTH), BF16)],
        compiler_params=_cparams("arbitrary"),
        name="ffn_inproj1",
    )(x, *ffn_w, g2, w)


def _lru_terms(xb, wa_ref, ba_ref, wi_ref, bi_ref, lam_ref):
    xg = xb.astype(BF16)
    ra, ia = [], []
    for g in range(LRU_BLOCKS):
        blk = xg[:, g * LRU_BLOCK_DIM:(g + 1) * LRU_BLOCK_DIM]
        ra.append(jnp.dot(blk, wa_ref[g], preferred_element_type=F32))
        ia.append(jnp.dot(blk, wi_ref[g], preferred_element_type=F32))
    r = _sigmoid(jnp.concatenate(ra, axis=1) + ba_ref[...])
    i = _sigmoid(jnp.concatenate(ia, axis=1) + bi_ref[...])
    z = -lam_ref[...]
    softplus = jnp.maximum(z, 0.0) + jnp.log(1.0 + jnp.exp(-jnp.abs(z)))
    a = jnp.exp2(r * ((-LRU_C * LOG2E) * softplus))
    t = 1.0 - a * a
    root = jnp.where(t == 0.0, 0.0, t * lax.rsqrt(t))
    b = root * i * xb
    return a, b


def _chunk_rows(ref, j):
    return jnp.concatenate([ref[s, pl.ds(j, SUBLANES, stride=CHUNK_PITCH), :] for s in range(LRU_SLABS)], axis=1)


def _chunked_conv(cur_ref, prev_ref, next_ref, cw_ref, cb_ref, first, last, xs_ref):
    cur = cur_ref[...]
    prev = jnp.where(first, 0.0, prev_ref[...])
    nxt = jnp.where(last, 0.0, next_ref[...])
    for c in range(SUBLANES):
        follow = cur[(c + 1) * CHUNK:(c + 1) * CHUNK + SUBLANES] if c + 1 < SUBLANES else nxt
        for s in range(LRU_SLABS):
            lanes = slice(s * LANES, (s + 1) * LANES)
            xs_ref[s, c * CHUNK_PITCH:c * CHUNK_PITCH + CHUNK, :] = cur[c * CHUNK:(c + 1) * CHUNK, lanes]
            xs_ref[s, c * CHUNK_PITCH + CHUNK:(c + 1) * CHUNK_PITCH, :] = follow[:, lanes]
    x = {j: _chunk_rows(xs_ref, j) for j in range(CHUNK + 1)}
    sub = lax.broadcasted_iota(jnp.int32, (SUBLANES, LRU_WIDTH), 0)
    for back in (1, 2):
        x[-back] = jnp.where(sub == 0, prev[SUBLANES - back:SUBLANES - back + 1, :],
                             pltpu.roll(x[CHUNK - back], 1, axis=0))
    w = [jnp.broadcast_to(cw_ref[t:t + 1, :], (SUBLANES, LRU_WIDTH)) for t in range(4)]
    bias = jnp.broadcast_to(cb_ref[...], (SUBLANES, LRU_WIDTH))
    return jnp.concatenate(
        [w[0] * x[j - 2] + w[1] * x[j - 1] + w[2] * x[j] + w[3] * x[j + 1] + bias for j in range(CHUNK)], axis=0)


def _scan_tile(a, b, h_ref, carry, reverse):
    steps = range(CHUNK - 1, -1, -1) if reverse else range(CHUNK)
    local, prod = [None] * CHUNK, [None] * CHUNK
    h = p = None
    for j in steps:
        aj, bj = a[j * SUBLANES:(j + 1) * SUBLANES], b[j * SUBLANES:(j + 1) * SUBLANES]
        h = bj if h is None else aj * h + bj
        p = aj if p is None else aj * p
        local[j], prod[j] = h, p

    chunk_in = [None] * SUBLANES
    c_state = carry
    for c in (range(SUBLANES - 1, -1, -1) if reverse else range(SUBLANES)):
        chunk_in[c] = c_state
        c_state = h[c:c + 1, :] + p[c:c + 1, :] * c_state
    chunk_in = jnp.concatenate(chunk_in, axis=0)

    for j in range(CHUNK):
        full = local[j] + prod[j] * chunk_in
        for s in range(LRU_SLABS):
            h_ref[s, pl.ds(j, SUBLANES, stride=CHUNK_PITCH), :] = full[:, s * LANES:(s + 1) * LANES]
    out = jnp.concatenate(
        [jnp.concatenate([h_ref[s, c * CHUNK_PITCH:c * CHUNK_PITCH + CHUNK, :] for c in range(SUBLANES)], axis=0)
         for s in range(LRU_SLABS)], axis=1)
    return out, c_state


def _lru_kernel(xf_ref, xfp_ref, xfn_ref, xr_ref, xrp_ref, xrn_ref, cw_ref, cb_ref,
                fwa_ref, fba_ref, fwi_ref, fbi_ref, flam_ref, bwa_ref, bba_ref, bwi_ref, bbi_ref, blam_ref,
                hf_ref, hb_ref, cf_ref, cr_ref, xs_ref, h_ref):
    i = pl.program_id(1)

    @pl.when(i == 0)
    def _():
        cf_ref[...] = jnp.zeros_like(cf_ref)
        cr_ref[...] = jnp.zeros_like(cr_ref)

    def conv(cur_ref, prev_ref, next_ref, tile):
        return _chunked_conv(cur_ref, prev_ref, next_ref, cw_ref, cb_ref, tile == 0, tile == SCAN_TILES - 1, xs_ref)

    a, b = _lru_terms(conv(xf_ref, xfp_ref, xfn_ref, i), fwa_ref, fba_ref, fwi_ref, fbi_ref, flam_ref)
    h, carry = _scan_tile(a, b, h_ref, cf_ref[0:1, :], reverse=False)
    hf_ref[...] = h.astype(BF16)
    cf_ref[0:1, :] = carry

    a, b = _lru_terms(conv(xr_ref, xrp_ref, xrn_ref, SCAN_TILES - 1 - i), bwa_ref, bba_ref, bwi_ref, bbi_ref, blam_ref)
    h, carry = _scan_tile(a, b, h_ref, cr_ref[0:1, :], reverse=True)
    hb_ref[...] = h.astype(BF16)
    cr_ref[0:1, :] = carry


def _lru(xb, conv_w, conv_b, fwd, bwd):
    hb = TS // SUBLANES
    sblk = SEQ // SUBLANES
    fidx = lambda b, i: b * SCAN_TILES + i
    ridx = lambda b, i: b * SCAN_TILES + SCAN_TILES - 1 - i

    def specs(tile_of):
        local = lambda b, i: tile_of(b, i) - b * SCAN_TILES
        return [pl.BlockSpec((TS, LRU_WIDTH), lambda b, i: (tile_of(b, i), 0)),
                pl.BlockSpec((SUBLANES, LRU_WIDTH), lambda b, i: (b * sblk + jnp.maximum(local(b, i) * hb - 1, 0), 0)),
                pl.BlockSpec((SUBLANES, LRU_WIDTH),
                             lambda b, i: (b * sblk + jnp.minimum((local(b, i) + 1) * hb, sblk - 1), 0))]

    wspec = [_const_spec((LRU_BLOCKS, LRU_BLOCK_DIM, LRU_BLOCK_DIM)), _const_spec((1, LRU_WIDTH)),
             _const_spec((LRU_BLOCKS, LRU_BLOCK_DIM, LRU_BLOCK_DIM)), _const_spec((1, LRU_WIDTH)),
             _const_spec((1, LRU_WIDTH))]
    out = jax.ShapeDtypeStruct((T_TOK, LRU_WIDTH), BF16)
    scan_scratch = pltpu.VMEM((LRU_SLABS, SUBLANES * CHUNK_PITCH, LANES), F32)
    return pl.pallas_call(
        _lru_kernel,
        grid=(N_SEQ, SCAN_TILES),
        in_specs=specs(fidx) + specs(ridx) + [_const_spec(conv_w.shape), _const_spec((1, LRU_WIDTH))] + wspec + wspec,
        out_specs=[pl.BlockSpec((TS, LRU_WIDTH), lambda b, i: (fidx(b, i), 0)),
                   pl.BlockSpec((TS, LRU_WIDTH), lambda b, i: (ridx(b, i), 0))],
        out_shape=[out, out],
        scratch_shapes=[pltpu.VMEM((SUBLANES, LRU_WIDTH), F32), pltpu.VMEM((SUBLANES, LRU_WIDTH), F32),
                        scan_scratch, scan_scratch],
        compiler_params=_cparams("arbitrary", "arbitrary"),
        name="lru_scan",
    )(xb, xb, xb, xb, xb, xb, conv_w, conv_b, *fwd, *bwd)


def _outproj1_ffn_kernel(x_ref, hf_ref, hb_ref, gate_ref, w_ref, g_ref, wg_ref, wu_ref, wd_ref, fn_ref, out_ref):
    y = (hf_ref[...].astype(F32) + hb_ref[...].astype(F32)) * gate_ref[...].astype(F32)
    x = x_ref[...] + jnp.dot(y.astype(BF16), w_ref[...], preferred_element_type=F32)
    out_ref[...] = _rmsnorm(_ffn_math(x, g_ref, wg_ref, wu_ref, wd_ref), fn_ref[...])


def _outproj1_ffn(x, hf, hb, gate, w, ffn_w, fn, tile0, n_tiles):
    row = pl.BlockSpec((TM, D_MODEL), lambda i: (i + tile0, 0))
    return pl.pallas_call(
        _outproj1_ffn_kernel,
        grid=(n_tiles,),
        in_specs=[row, row, row, row, _const_spec(w.shape)] + _ffn_weight_specs() + [_const_spec((1, D_MODEL))],
        out_specs=pl.BlockSpec((TM, D_MODEL), lambda i: (i, 0)),
        out_shape=jax.ShapeDtypeStruct((n_tiles * TM, D_MODEL), F32),
        compiler_params=_cparams("arbitrary"),
        name="outproj1_ffn",
    )(x, hf, hb, gate, w, *ffn_w, fn)


def _rope_freq_lanes():
    inv_freq = ROPE_THETA ** (-jnp.arange(ROPE_HALF, dtype=F32) / ROPE_HALF)
    dim = jnp.arange(LANES) % HEAD_DIM
    return jnp.where(dim < ROPE_DIM, inv_freq[dim % ROPE_HALF], 0.0).astype(F32).reshape(1, LANES)


def _head_expand_matrix():
    src = jnp.arange(2 * LANES) % LANES
    dst = jnp.arange(ATTN_WIDTH) // HEAD_DIM
    return (src[:, None] == dst[None, :]).astype(BF16)


def kernel(x_prompt, x_sample, l0_ffn1_norm, l0_ffn1_w_gate, l0_ffn1_w_up, l0_ffn1_w_down, l0_mix_norm, l0_w_in, l0_conv_w, l0_w_out, l0_ffn2_norm, l0_ffn2_w_gate, l0_ffn2_w_up, l0_ffn2_w_down, l1_ffn1_norm, l1_ffn1_w_gate, l1_ffn1_w_up, l1_ffn1_w_down, l1_mix_norm, l1_w_in, l1_conv_w, l1_conv_b, l1_fwd_w_a, l1_fwd_b_a, l1_fwd_w_i, l1_fwd_b_i, l1_fwd_lambda, l1_bwd_w_a, l1_bwd_b_a, l1_bwd_w_i, l1_bwd_b_i, l1_bwd_lambda, l1_w_out, l1_ffn2_norm, l1_ffn2_w_gate, l1_ffn2_w_up, l1_ffn2_w_down, final_norm):
    vec = lambda t: t.reshape(1, -1).astype(F32)
    wt = lambda t: t.astype(BF16)
    ffn_w = lambda g, w_gate, w_up, w_down: (vec(g), wt(w_gate), wt(w_up), wt(w_down))
    x_parts = (x_prompt.reshape(-1, D_MODEL), x_sample.reshape(-1, D_MODEL))

    x = _ffn(x_parts, ffn_w(l0_ffn1_norm, l0_ffn1_w_gate, l0_ffn1_w_up, l0_ffn1_w_down))
    p, gb, *qkv = _inproj0(x, vec(l0_mix_norm), wt(l0_w_in), _rope_tables(_rope_freq_lanes()))
    branches = [_attention_branch(*qkv[3 * n:3 * n + 3], dil) for n, dil in enumerate(DILATIONS)]
    x = _outproj0_ffn(x, p, gb, [o for o, _ in branches], [l for _, l in branches], l0_conv_w.astype(F32),
                      _head_expand_matrix(), wt(l0_w_out),
                      ffn_w(l0_ffn2_norm, l0_ffn2_w_gate, l0_ffn2_w_up, l0_ffn2_w_down))

    x, xb, gate = _ffn_inproj1(x, ffn_w(l1_ffn1_norm, l1_ffn1_w_gate, l1_ffn1_w_up, l1_ffn1_w_down),
                               vec(l1_mix_norm), wt(l1_w_in))
    fwd = (wt(l1_fwd_w_a), vec(l1_fwd_b_a), wt(l1_fwd_w_i), vec(l1_fwd_b_i), vec(l1_fwd_lambda))
    bwd = (wt(l1_bwd_w_a), vec(l1_bwd_b_a), wt(l1_bwd_w_i), vec(l1_bwd_b_i), vec(l1_bwd_lambda))
    hf, hb = _lru(xb, l1_conv_w.astype(F32), vec(l1_conv_b), fwd, bwd)
    last = (x, hf, hb, gate, wt(l1_w_out), ffn_w(l1_ffn2_norm, l1_ffn2_w_gate, l1_ffn2_w_up, l1_ffn2_w_down),
            vec(final_norm))
    prompt_tiles = x_parts[0].shape[0] // TM
    y_prompt = _outproj1_ffn(*last, tile0=0, n_tiles=prompt_tiles)
    y_sample = _outproj1_ffn(*last, tile0=prompt_tiles, n_tiles=N_TILES - prompt_tiles)
    return (y_prompt.reshape(x_prompt.shape), y_sample.reshape(x_sample.shape))
```

```python
import functools
import math

import jax
import jax.numpy as jnp
from jax import lax
from jax.experimental import pallas as pl
from jax.experimental.pallas import tpu as pltpu

F32 = jnp.float32
BF16 = jnp.bfloat16

D_MODEL = 1024
SEQ = 16384
N_SEQ = 3
T_TOK = N_SEQ * SEQ
D_FF = 2816
EPS = 1e-6
MASK_VALUE = -1e30
LOG2E = math.log2(math.e)

CONV_WIDTH = 512
ATTN_HEADS = 8
HEAD_DIM = 64
ATTN_WIDTH = ATTN_HEADS * HEAD_DIM
ROPE_DIM = HEAD_DIM // 4
ROPE_HALF = ROPE_DIM // 2
ROPE_THETA = 500000.0
DILATIONS = (1, 4, 16)
HALF_WINDOW = 64
LRU_WIDTH = 1024
LRU_BLOCKS = 4
LRU_BLOCK_DIM = LRU_WIDTH // LRU_BLOCKS
LRU_C = 8.0

LANES = 128
SUBLANES = 8
BF16_ROWS = 16
VMEM_LIMIT = 56 * 1024 * 1024

TM = 512
TILES_PER_SEQ = SEQ // TM
N_TILES = T_TOK // TM
TF = 1024
FF_CHUNKS = ((0, 1536), (1536, 1280))
ATTN_SLABS = ATTN_WIDTH // LANES
TQ = 1024
QB = 128
KW = QB + 2 * HALF_WINDOW
TS = 512
SCAN_TILES = SEQ // TS
LRU_SLABS = LRU_WIDTH // LANES
CHUNK = TS // SUBLANES
CHUNK_PITCH = CHUNK + SUBLANES


def _cparams(*sem):
    return pltpu.CompilerParams(dimension_semantics=sem, vmem_limit_bytes=VMEM_LIMIT)


def _const_spec(shape):
    nd = len(shape)
    return pl.BlockSpec(shape, lambda *_: (0,) * nd, pipeline_mode=pl.Buffered(1))


def _rmsnorm(x, g):
    ms = jnp.mean(x * x, axis=-1, keepdims=True)
    return x * lax.rsqrt(ms + EPS) * g


def _sigmoid(x):
    return 1.0 / (1.0 + jnp.exp2(x * (-LOG2E)))


def _ffn_math(x, g_ref, wg_ref, wu_ref, wd_ref):
    h = _rmsnorm(x, g_ref[...]).astype(BF16)
    acc = None
    for c0, cw in FF_CHUNKS:
        g = jnp.dot(h, wg_ref[:, c0:c0 + cw], preferred_element_type=F32)
        u = jnp.dot(h, wu_ref[:, c0:c0 + cw], preferred_element_type=F32)
        a = (g * _sigmoid(g) * u).astype(BF16)
        y = jnp.dot(a, wd_ref[c0:c0 + cw, :], preferred_element_type=F32)
        acc = y if acc is None else acc + y
    return x + 0.5 * acc


def _ffn_weight_specs():
    return [_const_spec((1, D_MODEL)), _const_spec((D_MODEL, D_FF)), _const_spec((D_MODEL, D_FF)),
            _const_spec((D_FF, D_MODEL))]


def _ffn_kernel(xa_ref, xb_ref, g_ref, wg_ref, wu_ref, wd_ref, o_ref, *, split):
    x = jnp.where(pl.program_id(0) < split, xa_ref[...], xb_ref[...])
    o_ref[...] = _ffn_math(x, g_ref, wg_ref, wu_ref, wd_ref)


def _ffn(x_parts, ffn_w):
    split = x_parts[0].shape[0] // TF
    x_specs = [pl.BlockSpec((TF, D_MODEL), lambda i: (jnp.minimum(i, split - 1), 0)),
               pl.BlockSpec((TF, D_MODEL), lambda i: (jnp.maximum(i - split, 0), 0))]
    return pl.pallas_call(
        functools.partial(_ffn_kernel, split=split),
        grid=(T_TOK // TF,),
        in_specs=x_specs + _ffn_weight_specs(),
        out_specs=pl.BlockSpec((TF, D_MODEL), lambda i: (i, 0)),
        out_shape=jax.ShapeDtypeStruct((T_TOK, D_MODEL), F32),
        compiler_params=_cparams("arbitrary"),
        name="ffn",
    )(*x_parts, *ffn_w)


def _rope(x, cos, sin_lo, sin_hi):
    return (x * cos + pltpu.roll(x, ATTN_WIDTH - ROPE_HALF, axis=1) * sin_lo
            + pltpu.roll(x, ROPE_HALF, axis=1) * sin_hi)


def _rope_table_kernel(freq_ref, cos_ref, sinlo_ref, sinhi_ref):
    pos = (pl.program_id(0) * TM + lax.broadcasted_iota(jnp.int32, (TM, LANES), 0)).astype(F32)
    ang = pos * freq_ref[...]
    sin = jnp.sin(ang)
    low = (lax.broadcasted_iota(jnp.int32, (TM, LANES), 1) & ROPE_HALF) == 0
    cos_ref[...] = jnp.cos(ang)
    sinlo_ref[...] = jnp.where(low, -sin, 0.0)
    sinhi_ref[...] = jnp.where(low, 0.0, sin)


def _rope_tables(freq):
    tab = pl.BlockSpec((TM, LANES), lambda i: (i, 0))
    shape = jax.ShapeDtypeStruct((SEQ, LANES), F32)
    return pl.pallas_call(
        _rope_table_kernel,
        grid=(TILES_PER_SEQ,),
        in_specs=[_const_spec((1, LANES))],
        out_specs=[tab, tab, tab],
        out_shape=[shape, shape, shape],
        compiler_params=_cparams("arbitrary"),
        name="rope_tables",
    )(freq)


def _inproj0_kernel(x_ref, g_ref, w_ref, cos_ref, sinlo_ref, sinhi_ref, p_ref, gb_ref,
                    q1_ref, k1_ref, v1_ref, q4_ref, k4_ref, v4_ref, q16_ref, k16_ref, v16_ref, slab_ref, slab4_ref):
    h = _rmsnorm(x_ref[...], g_ref[...]).astype(BF16)
    c = CONV_WIDTH
    proj = lambda j: jnp.dot(h, w_ref[:, j * c:(j + 1) * c], preferred_element_type=F32)
    q, k, v = proj(3), proj(4), proj(5)
    u, gb, gc = proj(0), proj(1), proj(2)
    p_ref[...] = (gc * u).astype(BF16)
    gb_ref[...] = gb.astype(BF16)

    cos = jnp.concatenate([cos_ref[...]] * ATTN_SLABS, axis=1)
    sin_lo = jnp.concatenate([sinlo_ref[...]] * ATTN_SLABS, axis=1)
    sin_hi = jnp.concatenate([sinhi_ref[...]] * ATTN_SLABS, axis=1)
    q = _rope(q, cos, sin_lo, sin_hi) * (HEAD_DIM ** -0.5)
    k = _rope(k, cos, sin_lo, sin_hi)

    outs = ((q, q1_ref, q4_ref, q16_ref), (k, k1_ref, k4_ref, k16_ref), (v, v1_ref, v4_ref, v16_ref))
    for a, (val, nat_ref, d4_ref, d16_ref) in enumerate(outs):
        nat_ref[...] = val.astype(BF16)
        for s in range(ATTN_SLABS):
            slab_ref[a * ATTN_SLABS + s] = val[:, s * LANES:(s + 1) * LANES]
        for r4 in range(4):
            cls4 = [slab_ref[a * ATTN_SLABS + s, pl.ds(r4, TM // 4, stride=4), :] for s in range(ATTN_SLABS)]
            d4_ref[:, r4 * ATTN_WIDTH:(r4 + 1) * ATTN_WIDTH] = jnp.concatenate(cls4, axis=1).astype(BF16)
            for s in range(ATTN_SLABS):
                slab4_ref[s] = cls4[s]
            for hi in range(4):
                cls16 = [slab4_ref[s, pl.ds(hi, TM // 16, stride=4), :] for s in range(ATTN_SLABS)]
                r16 = 4 * hi + r4
                d16_ref[:, r16 * ATTN_WIDTH:(r16 + 1) * ATTN_WIDTH] = jnp.concatenate(cls16, axis=1).astype(BF16)


def _inproj0(x, g, w, tables):
    row = pl.BlockSpec((TM, D_MODEL), lambda i: (i, 0))
    tab = pl.BlockSpec((TM, LANES), lambda i: (i % TILES_PER_SEQ, 0))
    half = pl.BlockSpec((TM, CONV_WIDTH), lambda i: (i, 0))
    nat = jax.ShapeDtypeStruct((T_TOK, CONV_WIDTH), BF16)
    lay_specs, lay_shapes = [], []
    for dil in DILATIONS:
        lay_specs += [pl.BlockSpec((TM // dil, dil * ATTN_WIDTH), lambda i: (i, 0))] * 3
        lay_shapes += [jax.ShapeDtypeStruct((T_TOK // dil, dil * ATTN_WIDTH), BF16)] * 3
    return pl.pallas_call(
        _inproj0_kernel,
        grid=(N_TILES,),
        in_specs=[row, _const_spec((1, D_MODEL)), _const_spec(w.shape), tab, tab, tab],
        out_specs=[half, half] + lay_specs,
        out_shape=[nat, nat] + lay_shapes,
        scratch_shapes=[pltpu.VMEM((3 * ATTN_SLABS, TM, LANES), F32), pltpu.VMEM((ATTN_SLABS, TM // 4, LANES), F32)],
        compiler_params=_cparams("arbitrary"),
        name="inproj0",
    )(x, g, w, *tables)


def _attn_kernel(q_ref, km_ref, kp_ref, kn_ref, vm_ref, vp_ref, vn_ref, o_ref, lse_ref, kw_ref, vw_ref, *, seq_len):
    hw = HALF_WINDOW
    kw_ref[0:hw, :] = kp_ref[...]
    kw_ref[hw:hw + TQ, :] = km_ref[...]
    kw_ref[hw + TQ:, :] = kn_ref[...]
    vw_ref[0:hw, :] = vp_ref[...]
    vw_ref[hw:hw + TQ, :] = vm_ref[...]
    vw_ref[hw + TQ:, :] = vn_ref[...]

    tile0 = pl.program_id(2) * TQ
    row = lax.broadcasted_iota(jnp.int32, (QB, KW), 0)
    col = lax.broadcasted_iota(jnp.int32, (QB, KW), 1)
    head_lane = lax.broadcasted_iota(jnp.int32, (QB, 2 * LANES), 1) // HEAD_DIM
    head_mask = [(head_lane == c).astype(F32).astype(BF16) for c in range(4)]
    stat_lane = lax.broadcasted_iota(jnp.int32, (QB, LANES), 1)
    halves = [slice(h * 2 * LANES, (h + 1) * 2 * LANES) for h in range(2)]

    band = (col >= row) & (col <= row + 2 * hw)

    def body(j, carry):
        q0 = tile0 + j * QB
        off = pl.multiple_of(j * QB, QB)
        valid = band & (col >= hw - q0) & (col <= seq_len - 1 + hw - q0)
        qb = q_ref[pl.ds(off, QB), :]
        kwin = kw_ref[pl.ds(off, KW), :]
        vwin = vw_ref[pl.ds(off, KW), :]
        scores = []
        for half in range(2):
            qh = jnp.concatenate([qb[:, halves[half]] * head_mask[c] for c in range(4)], axis=0)
            s = lax.dot_general(qh, kwin[:, halves[half]], (((1,), (1,)), ((), ())), preferred_element_type=F32)
            scores += [jnp.where(valid, s[c * QB:(c + 1) * QB], MASK_VALUE) for c in range(4)]
        probs, inv_den = [], []
        lse_all = jnp.zeros((QB, LANES), F32)
        for head in range(ATTN_HEADS):
            s = scores[head]
            m = jnp.max(s, axis=-1, keepdims=True)
            p = jnp.exp(s - m)
            den = jnp.sum(p, axis=-1, keepdims=True)
            probs.append(p.astype(BF16))
            inv_den.append(1.0 / den)
            lse_all = jnp.where(stat_lane == head, m + jnp.log(den), lse_all)
        for half in range(2):
            pv = jnp.dot(jnp.concatenate(probs[half * 4:half * 4 + 4], axis=0), vwin[:, halves[half]],
                         preferred_element_type=F32)
            acc = jnp.zeros((QB, 2 * LANES), F32)
            for c in range(4):
                acc = jnp.where(head_lane == c, pv[c * QB:(c + 1) * QB] * inv_den[half * 4 + c], acc)
            o_ref[pl.ds(off, QB), halves[half]] = acc.astype(BF16)
        lse_ref[pl.ds(off, QB), :] = lse_all
        return carry

    lax.fori_loop(0, TQ // QB, body, 0, unroll=True)


def _attention_branch(q, k, v, dil):
    seq_len = SEQ // dil
    rows = N_SEQ * seq_len
    tiles = seq_len // TQ
    hblk = seq_len // HALF_WINDOW
    main = pl.BlockSpec((TQ, ATTN_WIDTH), lambda b, r, j: (b * tiles + j, r))
    prev = pl.BlockSpec((HALF_WINDOW, ATTN_WIDTH),
                        lambda b, r, j: (b * hblk + jnp.maximum(j * (TQ // HALF_WINDOW) - 1, 0), r))
    nxt = pl.BlockSpec((HALF_WINDOW, ATTN_WIDTH),
                       lambda b, r, j: (b * hblk + jnp.minimum((j + 1) * (TQ // HALF_WINDOW), hblk - 1), r))
    stat = pl.BlockSpec((TQ, LANES), lambda b, r, j: (b * tiles + j, r))
    return pl.pallas_call(
        functools.partial(_attn_kernel, seq_len=seq_len),
        grid=(N_SEQ, dil, tiles),
        in_specs=[main, main, prev, nxt, main, prev, nxt],
        out_specs=[main, stat],
        out_shape=[jax.ShapeDtypeStruct((rows, dil * ATTN_WIDTH), BF16),
                   jax.ShapeDtypeStruct((rows, dil * LANES), F32)],
        scratch_shapes=[pltpu.VMEM((TQ + 2 * HALF_WINDOW, ATTN_WIDTH), BF16),
                        pltpu.VMEM((TQ + 2 * HALF_WINDOW, ATTN_WIDTH), BF16)],
        compiler_params=_cparams("arbitrary", "arbitrary", "arbitrary"),
        name=f"attn_d{dil}",
    )(q, k, k, k, v, v, v)


def _shifted(prev, cur, nxt, shift):
    ext = jnp.concatenate([prev, cur, nxt], axis=0)
    n = ext.shape[0]
    lo = prev.shape[0]
    return pltpu.roll(ext, shift % n, axis=0)[lo:lo + cur.shape[0]]


def _token_order(o_ref, l_ref, oslab_ref, lslab_ref, dil):
    rows = TM // dil
    for r in range(dil):
        lslab_ref[pl.ds(r, rows, stride=dil), :] = l_ref[:, r * LANES:(r + 1) * LANES]
        for s in range(ATTN_SLABS):
            c0 = r * ATTN_WIDTH + s * LANES
            oslab_ref[s, pl.ds(r, rows, stride=dil), :] = o_ref[:, c0:c0 + LANES].astype(F32)
    return jnp.concatenate([oslab_ref[s] for s in range(ATTN_SLABS)], axis=1), lslab_ref[...]


def _outproj0_ffn_kernel(x_ref, p_ref, pp_ref, pn_ref, gb_ref, o1_ref, o4_ref, o16_ref, l1_ref, l4_ref, l16_ref,
                         cw_ref, ex_ref, w_ref, g_ref, wg_ref, wu_ref, wd_ref, out_ref,
                         os4_ref, ls4_ref, os16_ref, ls16_ref):
    tile = pl.program_id(0) % TILES_PER_SEQ
    first = (tile == 0)
    last = (tile == TILES_PER_SEQ - 1)
    cur = p_ref[...].astype(F32)
    prev = jnp.where(first, 0.0, pp_ref[...].astype(F32))
    nxt = jnp.where(last, 0.0, pn_ref[...].astype(F32))
    cw = cw_ref[...]
    conv = (cw[0:1, :] * _shifted(prev, cur, nxt, 1) + cw[1:2, :] * cur + cw[2:3, :] * _shifted(prev, cur, nxt, -1))
    ya = gb_ref[...].astype(F32) * conv

    o1, l1 = o1_ref[...].astype(F32), l1_ref[...]
    o2, l2 = _token_order(o4_ref, l4_ref, os4_ref, ls4_ref, 4)
    o3, l3 = _token_order(o16_ref, l16_ref, os16_ref, ls16_ref, 16)
    m = jnp.maximum(jnp.maximum(l1, l2), l3)
    e1, e2, e3 = jnp.exp(l1 - m), jnp.exp(l2 - m), jnp.exp(l3 - m)
    tot = e1 + e2 + e3
    yb = None
    for e, o in ((e1, o1), (e2, o2), (e3, o3)):
        wgt = e / tot
        hi = wgt.astype(BF16)
        lo = (wgt - hi.astype(F32)).astype(BF16)
        wide = jnp.dot(jnp.concatenate([hi, lo], axis=1), ex_ref[...], preferred_element_type=F32)
        term = wide * o
        yb = term if yb is None else yb + term
    y = jnp.concatenate([ya.astype(BF16), yb.astype(BF16)], axis=1)
    x = x_ref[...] + jnp.dot(y, w_ref[...], preferred_element_type=F32)
    out_ref[...] = _ffn_math(x, g_ref, wg_ref, wu_ref, wd_ref)


def _outproj0_ffn(x, p, gb, outs, lses, conv_w, expand, w, ffn_w):
    row = pl.BlockSpec((TM, D_MODEL), lambda i: (i, 0))
    half = pl.BlockSpec((TM, CONV_WIDTH), lambda i: (i, 0))
    hb = TM // BF16_ROWS
    nblk = T_TOK // BF16_ROWS
    prev = pl.BlockSpec((BF16_ROWS, CONV_WIDTH), lambda i: (jnp.maximum(i * hb - 1, 0), 0))
    nxt = pl.BlockSpec((BF16_ROWS, CONV_WIDTH), lambda i: (jnp.minimum((i + 1) * hb, nblk - 1), 0))
    o_specs = [pl.BlockSpec((TM // dil, dil * ATTN_WIDTH), lambda i: (i, 0)) for dil in DILATIONS]
    l_specs = [pl.BlockSpec((TM // dil, dil * LANES), lambda i: (i, 0)) for dil in DILATIONS]
    return pl.pallas_call(
        _outproj0_ffn_kernel,
        grid=(N_TILES,),
        in_specs=[row, half, prev, nxt, half] + o_specs + l_specs
                 + [_const_spec(conv_w.shape), _const_spec(expand.shape), _const_spec(w.shape)] + _ffn_weight_specs(),
        out_specs=row,
        out_shape=jax.ShapeDtypeStruct((T_TOK, D_MODEL), F32),
        scratch_shapes=[pltpu.VMEM((ATTN_SLABS, TM, LANES), F32), pltpu.VMEM((TM, LANES), F32),
                        pltpu.VMEM((ATTN_SLABS, TM, LANES), F32), pltpu.VMEM((TM, LANES), F32)],
        compiler_params=_cparams("arbitrary"),
        name="outproj0_ffn",
    )(x, p, p, p, gb, *outs, *lses, conv_w, expand, w, *ffn_w)


def _ffn_inproj1_kernel(x_ref, g_ref, wg_ref, wu_ref, wd_ref, g2_ref, w_ref, x_out_ref, xb_ref, gate_ref):
    x = _ffn_math(x_ref[...], g_ref, wg_ref, wu_ref, wd_ref)
    x_out_ref[...] = x
    h = _rmsnorm(x, g2_ref[...]).astype(BF16)
    z = jnp.dot(h, w_ref[...], preferred_element_type=F32)
    xb_ref[...] = z[:, :LRU_WIDTH]
    gate = z[:, LRU_WIDTH:]
    inner = math.sqrt(2.0 / math.pi) * (gate + 0.044715 * (gate * gate * gate))
    gate_ref[...] = (0.5 * gate * (1.0 + jnp.tanh(inner))).astype(BF16)


def _ffn_inproj1(x, ffn_w, g2, w):
    row = pl.BlockSpec((TM, D_MODEL), lambda i: (i, 0))
    return pl.pallas_call(
        _ffn_inproj1_kernel,
        grid=(N_TILES,),
        in_specs=[row] + _ffn_weight_specs() + [_const_spec((1, D_MODEL)), _const_spec(w.shape)],
        out_specs=[row, row, row],
        out_shape=[jax.ShapeDtypeStruct((T_TOK, D_MODEL), F32), jax.ShapeDtypeStruct((T_TOK, LRU_WIDTH), F32),
                   jax.ShapeDtypeStruct((T_TOK, LRU_WIDTH), BF16)],
        compiler_params=_cparams("arbitrary"),
        name="ffn_inproj1",
    )(x, *ffn_w, g2, w)


def _lru_terms(xb, wa_ref, ba_ref, wi_ref, bi_ref, lam_ref):
    xg = xb.astype(BF16)
    ra, ia = [], []
    for g in range(LRU_BLOCKS):
        blk = xg[:, g * LRU_BLOCK_DIM:(g + 1) * LRU_BLOCK_DIM]
        ra.append(jnp.dot(blk, wa_ref[g], preferred_element_type=F32))
        ia.append(jnp.dot(blk, wi_ref[g], preferred_element_type=F32))
    r = _sigmoid(jnp.concatenate(ra, axis=1) + ba_ref[...])
    i = _sigmoid(jnp.concatenate(ia, axis=1) + bi_ref[...])
    z = -lam_ref[...]
    softplus = jnp.maximum(z, 0.0) + jnp.log(1.0 + jnp.exp(-jnp.abs(z)))
    a = jnp.exp2(r * ((-LRU_C * LOG2E) * softplus))
    t = 1.0 - a * a
    root = jnp.where(t == 0.0, 0.0, t * lax.rsqrt(t))
    b = root * i * xb
    return a, b


def _chunk_rows(ref, j):
    return jnp.concatenate([ref[s, pl.ds(j, SUBLANES, stride=CHUNK_PITCH), :] for s in range(LRU_SLABS)], axis=1)


def _chunked_conv(cur_ref, prev_ref, next_ref, cw_ref, cb_ref, first, last, xs_ref):
    cur = cur_ref[...]
    prev = jnp.where(first, 0.0, prev_ref[...])
    nxt = jnp.where(last, 0.0, next_ref[...])
    for c in range(SUBLANES):
        follow = cur[(c + 1) * CHUNK:(c + 1) * CHUNK + SUBLANES] if c + 1 < SUBLANES else nxt
        for s in range(LRU_SLABS):
            lanes = slice(s * LANES, (s + 1) * LANES)
            xs_ref[s, c * CHUNK_PITCH:c * CHUNK_PITCH + CHUNK, :] = cur[c * CHUNK:(c + 1) * CHUNK, lanes]
            xs_ref[s, c * CHUNK_PITCH + CHUNK:(c + 1) * CHUNK_PITCH, :] = follow[:, lanes]
    x = {j: _chunk_rows(xs_ref, j) for j in range(CHUNK + 1)}
    sub = lax.broadcasted_iota(jnp.int32, (SUBLANES, LRU_WIDTH), 0)
    for back in (1, 2):
        x[-back] = jnp.where(sub == 0, prev[SUBLANES - back:SUBLANES - back + 1, :],
                             pltpu.roll(x[CHUNK - back], 1, axis=0))
    w = [jnp.broadcast_to(cw_ref[t:t + 1, :], (SUBLANES, LRU_WIDTH)) for t in range(4)]
    bias = jnp.broadcast_to(cb_ref[...], (SUBLANES, LRU_WIDTH))
    return jnp.concatenate(
        [w[0] * x[j - 2] + w[1] * x[j - 1] + w[2] * x[j] + w[3] * x[j + 1] + bias for j in range(CHUNK)], axis=0)


def _scan_tile(a, b, h_ref, carry, reverse):
    steps = range(CHUNK - 1, -1, -1) if reverse else range(CHUNK)
    local, prod = [None] * CHUNK, [None] * CHUNK
    h = p = None
    for j in steps:
        aj, bj = a[j * SUBLANES:(j + 1) * SUBLANES], b[j * SUBLANES:(j + 1) * SUBLANES]
        h = bj if h is None else aj * h + bj
        p = aj if p is None else aj * p
        local[j], prod[j] = h, p

    chunk_in = [None] * SUBLANES
    c_state = carry
    for c in (range(SUBLANES - 1, -1, -1) if reverse else range(SUBLANES)):
        chunk_in[c] = c_state
        c_state = h[c:c + 1, :] + p[c:c + 1, :] * c_state
    chunk_in = jnp.concatenate(chunk_in, axis=0)

    for j in range(CHUNK):
        full = local[j] + prod[j] * chunk_in
        for s in range(LRU_SLABS):
            h_ref[s, pl.ds(j, SUBLANES, stride=CHUNK_PITCH), :] = full[:, s * LANES:(s + 1) * LANES]
    out = jnp.concatenate(
        [jnp.concatenate([h_ref[s, c * CHUNK_PITCH:c * CHUNK_PITCH + CHUNK, :] for c in range(SUBLANES)], axis=0)
         for s in range(LRU_SLABS)], axis=1)
    return out, c_state


def _lru_kernel(xf_ref, xfp_ref, xfn_ref, xr_ref, xrp_ref, xrn_ref, cw_ref, cb_ref,
                fwa_ref, fba_ref, fwi_ref, fbi_ref, flam_ref, bwa_ref, bba_ref, bwi_ref, bbi_ref, blam_ref,
                hf_ref, hb_ref, cf_ref, cr_ref, xs_ref, h_ref):
    i = pl.program_id(1)

    @pl.when(i == 0)
    def _():
        cf_ref[...] = jnp.zeros_like(cf_ref)
        cr_ref[...] = jnp.zeros_like(cr_ref)

    def conv(cur_ref, prev_ref, next_ref, tile):
        return _chunked_conv(cur_ref, prev_ref, next_ref, cw_ref, cb_ref, tile == 0, tile == SCAN_TILES - 1, xs_ref)

    a, b = _lru_terms(conv(xf_ref, xfp_ref, xfn_ref, i), fwa_ref, fba_ref, fwi_ref, fbi_ref, flam_ref)
    h, carry = _scan_tile(a, b, h_ref, cf_ref[0:1, :], reverse=False)
    hf_ref[...] = h.astype(BF16)
    cf_ref[0:1, :] = carry

    a, b = _lru_terms(conv(xr_ref, xrp_ref, xrn_ref, SCAN_TILES - 1 - i), bwa_ref, bba_ref, bwi_ref, bbi_ref, blam_ref)
    h, carry = _scan_tile(a, b, h_ref, cr_ref[0:1, :], reverse=True)
    hb_ref[...] = h.astype(BF16)
    cr_ref[0:1, :] = carry


def _lru(xb, conv_w, conv_b, fwd, bwd):
    hb = TS // SUBLANES
    sblk = SEQ // SUBLANES
    fidx = lambda b, i: b * SCAN_TILES + i
    ridx = lambda b, i: b * SCAN_TILES + SCAN_TILES - 1 - i

    def specs(tile_of):
        local = lambda b, i: tile_of(b, i) - b * SCAN_TILES
        return [pl.BlockSpec((TS, LRU_WIDTH), lambda b, i: (tile_of(b, i), 0)),
                pl.BlockSpec((SUBLANES, LRU_WIDTH), lambda b, i: (b * sblk + jnp.maximum(local(b, i) * hb - 1, 0), 0)),
                pl.BlockSpec((SUBLANES, LRU_WIDTH),
                             lambda b, i: (b * sblk + jnp.minimum((local(b, i) + 1) * hb, sblk - 1), 0))]

    wspec = [_const_spec((LRU_BLOCKS, LRU_BLOCK_DIM, LRU_BLOCK_DIM)), _const_spec((1, LRU_WIDTH)),
             _const_spec((LRU_BLOCKS, LRU_BLOCK_DIM, LRU_BLOCK_DIM)), _const_spec((1, LRU_WIDTH)),
             _const_spec((1, LRU_WIDTH))]
    out = jax.ShapeDtypeStruct((T_TOK, LRU_WIDTH), BF16)
    scan_scratch = pltpu.VMEM((LRU_SLABS, SUBLANES * CHUNK_PITCH, LANES), F32)
    return pl.pallas_call(
        _lru_kernel,
        grid=(N_SEQ, SCAN_TILES),
        in_specs=specs(fidx) + specs(ridx) + [_const_spec(conv_w.shape), _const_spec((1, LRU_WIDTH))] + wspec + wspec,
        out_specs=[pl.BlockSpec((TS, LRU_WIDTH), lambda b, i: (fidx(b, i), 0)),
                   pl.BlockSpec((TS, LRU_WIDTH), lambda b, i: (ridx(b, i), 0))],
        out_shape=[out, out],
        scratch_shapes=[pltpu.VMEM((SUBLANES, LRU_WIDTH), F32), pltpu.VMEM((SUBLANES, LRU_WIDTH), F32),
                        scan_scratch, scan_scratch],
        compiler_params=_cparams("arbitrary", "arbitrary"),
        name="lru_scan",
    )(xb, xb, xb, xb, xb, xb, conv_w, conv_b, *fwd, *bwd)


def _outproj1_ffn_kernel(x_ref, hf_ref, hb_ref, gate_ref, w_ref, g_ref, wg_ref, wu_ref, wd_ref, fn_ref, out_ref):
    y = (hf_ref[...].astype(F32) + hb_ref[...].astype(F32)) * gate_ref[...].astype(F32)
    x = x_ref[...] + jnp.dot(y.astype(BF16), w_ref[...], preferred_element_type=F32)
    out_ref[...] = _rmsnorm(_ffn_math(x, g_ref, wg_ref, wu_ref, wd_ref), fn_ref[...])


def _outproj1_ffn(x, hf, hb, gate, w, ffn_w, fn, tile0, n_tiles):
    row = pl.BlockSpec((TM, D_MODEL), lambda i: (i + tile0, 0))
    return pl.pallas_call(
        _outproj1_ffn_kernel,
        grid=(n_tiles,),
        in_specs=[row, row, row, row, _const_spec(w.shape)] + _ffn_weight_specs() + [_const_spec((1, D_MODEL))],
        out_specs=pl.BlockSpec((TM, D_MODEL), lambda i: (i, 0)),
        out_shape=jax.ShapeDtypeStruct((n_tiles * TM, D_MODEL), F32),
        compiler_params=_cparams("arbitrary"),
        name="outproj1_ffn",
    )(x, hf, hb, gate, w, *ffn_w, fn)


def _rope_freq_lanes():
    inv_freq = ROPE_THETA ** (-jnp.arange(ROPE_HALF, dtype=F32) / ROPE_HALF)
    dim = jnp.arange(LANES) % HEAD_DIM
    return jnp.where(dim < ROPE_DIM, inv_freq[dim % ROPE_HALF], 0.0).astype(F32).reshape(1, LANES)


def _head_expand_matrix():
    src = jnp.arange(2 * LANES) % LANES
    dst = jnp.arange(ATTN_WIDTH) // HEAD_DIM
    return (src[:, None] == dst[None, :]).astype(BF16)


def kernel(x_prompt, x_sample, l0_ffn1_norm, l0_ffn1_w_gate, l0_ffn1_w_up, l0_ffn1_w_down, l0_mix_norm, l0_w_in, l0_conv_w, l0_w_out, l0_ffn2_norm, l0_ffn2_w_gate, l0_ffn2_w_up, l0_ffn2_w_down, l1_ffn1_norm, l1_ffn1_w_gate, l1_ffn1_w_up, l1_ffn1_w_down, l1_mix_norm, l1_w_in, l1_conv_w, l1_conv_b, l1_fwd_w_a, l1_fwd_b_a, l1_fwd_w_i, l1_fwd_b_i, l1_fwd_lambda, l1_bwd_w_a, l1_bwd_b_a, l1_bwd_w_i, l1_bwd_b_i, l1_bwd_lambda, l1_w_out, l1_ffn2_norm, l1_ffn2_w_gate, l1_ffn2_w_up, l1_ffn2_w_down, final_norm):
    vec = lambda t: t.reshape(1, -1).astype(F32)
    wt = lambda t: t.astype(BF16)
    ffn_w = lambda g, w_gate, w_up, w_down: (vec(g), wt(w_gate), wt(w_up), wt(w_down))
    x_parts = (x_prompt.reshape(-1, D_MODEL), x_sample.reshape(-1, D_MODEL))

    x = _ffn(x_parts, ffn_w(l0_ffn1_norm, l0_ffn1_w_gate, l0_ffn1_w_up, l0_ffn1_w_down))
    p, gb, *qkv = _inproj0(x, vec(l0_mix_norm), wt(l0_w_in), _rope_tables(_rope_freq_lanes()))
    branches = [_attention_branch(*qkv[3 * n:3 * n + 3], dil) for n, dil in enumerate(DILATIONS)]
    x = _outproj0_ffn(x, p, gb, [o for o, _ in branches], [l for _, l in branches], l0_conv_w.astype(F32),
                      _head_expand_matrix(), wt(l0_w_out),
                      ffn_w(l0_ffn2_norm, l0_ffn2_w_gate, l0_ffn2_w_up, l0_ffn2_w_down))

    x, xb, gate = _ffn_inproj1(x, ffn_w(l1_ffn1_norm, l1_ffn1_w_gate, l1_ffn1_w_up, l1_ffn1_w_down),
                               vec(l1_mix_norm), wt(l1_w_in))
    fwd = (wt(l1_fwd_w_a), vec(l1_fwd_b_a), wt(l1_fwd_w_i), vec(l1_fwd_b_i), vec(l1_fwd_lambda))
    bwd = (wt(l1_bwd_w_a), vec(l1_bwd_b_a), wt(l1_bwd_w_i), vec(l1_bwd_b_i), vec(l1_bwd_lambda))
    hf, hb = _lru(xb, l1_conv_w.astype(F32), vec(l1_conv_b), fwd, bwd)
    last = (x, hf, hb, gate, wt(l1_w_out), ffn_w(l1_ffn2_norm, l1_ffn2_w_gate, l1_ffn2_w_up, l1_ffn2_w_down),
            vec(final_norm))
    prompt_tiles = x_parts[0].shape[0] // TM
    y_prompt = _outproj1_ffn(*last, tile0=0, n_tiles=prompt_tiles)
    y_sample = _outproj1_ffn(*last, tile0=prompt_tiles, n_tiles=N_TILES - prompt_tiles)
    return (y_prompt.reshape(x_prompt.shape), y_sample.reshape(x_sample.shape))
```

```python
import functools
import math

import jax
import jax.numpy as jnp
from jax import lax
from jax.experimental import pallas as pl
from jax.experimental.pallas import tpu as pltpu

F32 = jnp.float32
BF16 = jnp.bfloat16

D_MODEL = 1024
SEQ = 16384
N_SEQ = 3
T_TOK = N_SEQ * SEQ
D_FF = 2816
EPS = 1e-6
MASK_VALUE = -1e30
LOG2E = math.log2(math.e)

CONV_WIDTH = 512
ATTN_HEADS = 8
HEAD_DIM = 64
ATTN_WIDTH = ATTN_HEADS * HEAD_DIM
ROPE_DIM = HEAD_DIM // 4
ROPE_HALF = ROPE_DIM // 2
ROPE_THETA = 500000.0
DILATIONS = (1, 4, 16)
HALF_WINDOW = 64
LRU_WIDTH = 1024
LRU_BLOCKS = 4
LRU_BLOCK_DIM = LRU_WIDTH // LRU_BLOCKS
LRU_C = 8.0

LANES = 128
SUBLANES = 8
BF16_ROWS = 16
VMEM_LIMIT = 56 * 1024 * 1024

TM = 512
TILES_PER_SEQ = SEQ // TM
N_TILES = T_TOK // TM
TF = 1024
FF_CHUNKS = ((0, 1536), (1536, 1280))
ATTN_SLABS = ATTN_WIDTH // LANES
TQ = 1024
QB = 128
KW = QB + 2 * HALF_WINDOW
TS = 1024
SCAN_TILES = SEQ // TS
LRU_SLABS = LRU_WIDTH // LANES
CHUNK = TS // SUBLANES
CHUNK_PITCH = CHUNK + SUBLANES


def _cparams(*sem):
    return pltpu.CompilerParams(dimension_semantics=sem, vmem_limit_bytes=VMEM_LIMIT)


def _const_spec(shape):
    nd = len(shape)
    return pl.BlockSpec(shape, lambda *_: (0,) * nd, pipeline_mode=pl.Buffered(1))


def _rmsnorm(x, g):
    ms = jnp.mean(x * x, axis=-1, keepdims=True)
    return x * lax.rsqrt(ms + EPS) * g


def _sigmoid(x):
    return 1.0 / (1.0 + jnp.exp2(x * (-LOG2E)))


def _ffn_math(x, g_ref, wg_ref, wu_ref, wd_ref):
    h = _rmsnorm(x, g_ref[...]).astype(BF16)
    acc = None
    for c0, cw in FF_CHUNKS:
        g = jnp.dot(h, wg_ref[:, c0:c0 + cw], preferred_element_type=F32)
        u = jnp.dot(h, wu_ref[:, c0:c0 + cw], preferred_element_type=F32)
        a = (g * _sigmoid(g) * u).astype(BF16)
        y = jnp.dot(a, wd_ref[c0:c0 + cw, :], preferred_element_type=F32)
        acc = y if acc is None else acc + y
    return x + 0.5 * acc


def _ffn_weight_specs():
    return [_const_spec((1, D_MODEL)), _const_spec((D_MODEL, D_FF)), _const_spec((D_MODEL, D_FF)),
            _const_spec((D_FF, D_MODEL))]


def _ffn_kernel(xa_ref, xb_ref, g_ref, wg_ref, wu_ref, wd_ref, o_ref, *, split):
    x = jnp.where(pl.program_id(0) < split, xa_ref[...], xb_ref[...])
    o_ref[...] = _ffn_math(x, g_ref, wg_ref, wu_ref, wd_ref)


def _ffn(x_parts, ffn_w):
    split = x_parts[0].shape[0] // TF
    x_specs = [pl.BlockSpec((TF, D_MODEL), lambda i: (jnp.minimum(i, split - 1), 0)),
               pl.BlockSpec((TF, D_MODEL), lambda i: (jnp.maximum(i - split, 0), 0))]
    return pl.pallas_call(
        functools.partial(_ffn_kernel, split=split),
        grid=(T_TOK // TF,),
        in_specs=x_specs + _ffn_weight_specs(),
        out_specs=pl.BlockSpec((TF, D_MODEL), lambda i: (i, 0)),
        out_shape=jax.ShapeDtypeStruct((T_TOK, D_MODEL), F32),
        compiler_params=_cparams("arbitrary"),
        name="ffn",
    )(*x_parts, *ffn_w)


def _rope(x, cos, sin_lo, sin_hi):
    return (x * cos + pltpu.roll(x, ATTN_WIDTH - ROPE_HALF, axis=1) * sin_lo
            + pltpu.roll(x, ROPE_HALF, axis=1) * sin_hi)


def _rope_table_kernel(freq_ref, cos_ref, sinlo_ref, sinhi_ref):
    pos = (pl.program_id(0) * TM + lax.broadcasted_iota(jnp.int32, (TM, LANES), 0)).astype(F32)
    ang = pos * freq_ref[...]
    sin = jnp.sin(ang)
    low = (lax.broadcasted_iota(jnp.int32, (TM, LANES), 1) & ROPE_HALF) == 0
    cos_ref[...] = jnp.cos(ang)
    sinlo_ref[...] = jnp.where(low, -sin, 0.0)
    sinhi_ref[...] = jnp.where(low, 0.0, sin)


def _rope_tables(freq):
    tab = pl.BlockSpec((TM, LANES), lambda i: (i, 0))
    shape = jax.ShapeDtypeStruct((SEQ, LANES), F32)
    return pl.pallas_call(
        _rope_table_kernel,
        grid=(TILES_PER_SEQ,),
        in_specs=[_const_spec((1, LANES))],
        out_specs=[tab, tab, tab],
        out_shape=[shape, shape, shape],
        compiler_params=_cparams("arbitrary"),
        name="rope_tables",
    )(freq)


def _inproj0_kernel(x_ref, g_ref, w_ref, cos_ref, sinlo_ref, sinhi_ref, p_ref, gb_ref,
                    q1_ref, k1_ref, v1_ref, q4_ref, k4_ref, v4_ref, q16_ref, k16_ref, v16_ref, slab_ref, slab4_ref):
    h = _rmsnorm(x_ref[...], g_ref[...]).astype(BF16)
    c = CONV_WIDTH
    proj = lambda j: jnp.dot(h, w_ref[:, j * c:(j + 1) * c], preferred_element_type=F32)
    q, k, v = proj(3), proj(4), proj(5)
    u, gb, gc = proj(0), proj(1), proj(2)
    p_ref[...] = (gc * u).astype(BF16)
    gb_ref[...] = gb.astype(BF16)

    cos = jnp.concatenate([cos_ref[...]] * ATTN_SLABS, axis=1)
    sin_lo = jnp.concatenate([sinlo_ref[...]] * ATTN_SLABS, axis=1)
    sin_hi = jnp.concatenate([sinhi_ref[...]] * ATTN_SLABS, axis=1)
    q = _rope(q, cos, sin_lo, sin_hi) * (HEAD_DIM ** -0.5)
    k = _rope(k, cos, sin_lo, sin_hi)

    outs = ((q, q1_ref, q4_ref, q16_ref), (k, k1_ref, k4_ref, k16_ref), (v, v1_ref, v4_ref, v16_ref))
    for a, (val, nat_ref, d4_ref, d16_ref) in enumerate(outs):
        nat_ref[...] = val.astype(BF16)
        for s in range(ATTN_SLABS):
            slab_ref[a * ATTN_SLABS + s] = val[:, s * LANES:(s + 1) * LANES]
        for r4 in range(4):
            cls4 = [slab_ref[a * ATTN_SLABS + s, pl.ds(r4, TM // 4, stride=4), :] for s in range(ATTN_SLABS)]
            d4_ref[:, r4 * ATTN_WIDTH:(r4 + 1) * ATTN_WIDTH] = jnp.concatenate(cls4, axis=1).astype(BF16)
            for s in range(ATTN_SLABS):
                slab4_ref[s] = cls4[s]
            for hi in range(4):
                cls16 = [slab4_ref[s, pl.ds(hi, TM // 16, stride=4), :] for s in range(ATTN_SLABS)]
                r16 = 4 * hi + r4
                d16_ref[:, r16 * ATTN_WIDTH:(r16 + 1) * ATTN_WIDTH] = jnp.concatenate(cls16, axis=1).astype(BF16)


def _inproj0(x, g, w, tables):
    row = pl.BlockSpec((TM, D_MODEL), lambda i: (i, 0))
    tab = pl.BlockSpec((TM, LANES), lambda i: (i % TILES_PER_SEQ, 0))
    half = pl.BlockSpec((TM, CONV_WIDTH), lambda i: (i, 0))
    nat = jax.ShapeDtypeStruct((T_TOK, CONV_WIDTH), BF16)
    lay_specs, lay_shapes = [], []
    for dil in DILATIONS:
        lay_specs += [pl.BlockSpec((TM // dil, dil * ATTN_WIDTH), lambda i: (i, 0))] * 3
        lay_shapes += [jax.ShapeDtypeStruct((T_TOK // dil, dil * ATTN_WIDTH), BF16)] * 3
    return pl.pallas_call(
        _inproj0_kernel,
        grid=(N_TILES,),
        in_specs=[row, _const_spec((1, D_MODEL)), _const_spec(w.shape), tab, tab, tab],
        out_specs=[half, half] + lay_specs,
        out_shape=[nat, nat] + lay_shapes,
        scratch_shapes=[pltpu.VMEM((3 * ATTN_SLABS, TM, LANES), F32), pltpu.VMEM((ATTN_SLABS, TM // 4, LANES), F32)],
        compiler_params=_cparams("arbitrary"),
        name="inproj0",
    )(x, g, w, *tables)


def _attn_kernel(q_ref, km_ref, kp_ref, kn_ref, vm_ref, vp_ref, vn_ref, o_ref, lse_ref, kw_ref, vw_ref, *, seq_len):
    hw = HALF_WINDOW
    kw_ref[0:hw, :] = kp_ref[...]
    kw_ref[hw:hw + TQ, :] = km_ref[...]
    kw_ref[hw + TQ:, :] = kn_ref[...]
    vw_ref[0:hw, :] = vp_ref[...]
    vw_ref[hw:hw + TQ, :] = vm_ref[...]
    vw_ref[hw + TQ:, :] = vn_ref[...]

    tile0 = pl.program_id(2) * TQ
    row = lax.broadcasted_iota(jnp.int32, (QB, KW), 0)
    col = lax.broadcasted_iota(jnp.int32, (QB, KW), 1)
    head_lane = lax.broadcasted_iota(jnp.int32, (QB, 2 * LANES), 1) // HEAD_DIM
    head_mask = [(head_lane == c).astype(F32).astype(BF16) for c in range(4)]
    stat_lane = lax.broadcasted_iota(jnp.int32, (QB, LANES), 1)
    halves = [slice(h * 2 * LANES, (h + 1) * 2 * LANES) for h in range(2)]

    band = (col >= row) & (col <= row + 2 * hw)

    def body(j, carry):
        q0 = tile0 + j * QB
        off = pl.multiple_of(j * QB, QB)
        valid = band & (col >= hw - q0) & (col <= seq_len - 1 + hw - q0)
        qb = q_ref[pl.ds(off, QB), :]
        kwin = kw_ref[pl.ds(off, KW), :]
        vwin = vw_ref[pl.ds(off, KW), :]
        scores = []
        for half in range(2):
            qh = jnp.concatenate([qb[:, halves[half]] * head_mask[c] for c in range(4)], axis=0)
            s = lax.dot_general(qh, kwin[:, halves[half]], (((1,), (1,)), ((), ())), preferred_element_type=F32)
            scores += [jnp.where(valid, s[c * QB:(c + 1) * QB], MASK_VALUE) for c in range(4)]
        probs, inv_den = [], []
        lse_all = jnp.zeros((QB, LANES), F32)
        for head in range(ATTN_HEADS):
            s = scores[head]
            m = jnp.max(s, axis=-1, keepdims=True)
            p = jnp.exp(s - m)
            den = jnp.sum(p, axis=-1, keepdims=True)
            probs.append(p.astype(BF16))
            inv_den.append(1.0 / den)
            lse_all = jnp.where(stat_lane == head, m + jnp.log(den), lse_all)
        for half in range(2):
            pv = jnp.dot(jnp.concatenate(probs[half * 4:half * 4 + 4], axis=0), vwin[:, halves[half]],
                         preferred_element_type=F32)
            acc = jnp.zeros((QB, 2 * LANES), F32)
            for c in range(4):
                acc = jnp.where(head_lane == c, pv[c * QB:(c + 1) * QB] * inv_den[half * 4 + c], acc)
            o_ref[pl.ds(off, QB), halves[half]] = acc.astype(BF16)
        lse_ref[pl.ds(off, QB), :] = lse_all
        return carry

    lax.fori_loop(0, TQ // QB, body, 0, unroll=True)


def _attention_branch(q, k, v, dil):
    seq_len = SEQ // dil
    rows = N_SEQ * seq_len
    tiles = seq_len // TQ
    hblk = seq_len // HALF_WINDOW
    main = pl.BlockSpec((TQ, ATTN_WIDTH), lambda b, r, j: (b * tiles + j, r))
    prev = pl.BlockSpec((HALF_WINDOW, ATTN_WIDTH),
                        lambda b, r, j: (b * hblk + jnp.maximum(j * (TQ // HALF_WINDOW) - 1, 0), r))
    nxt = pl.BlockSpec((HALF_WINDOW, ATTN_WIDTH),
                       lambda b, r, j: (b * hblk + jnp.minimum((j + 1) * (TQ // HALF_WINDOW), hblk - 1), r))
    stat = pl.BlockSpec((TQ, LANES), lambda b, r, j: (b * tiles + j, r))
    return pl.pallas_call(
        functools.partial(_attn_kernel, seq_len=seq_len),
        grid=(N_SEQ, dil, tiles),
        in_specs=[main, main, prev, nxt, main, prev, nxt],
        out_specs=[main, stat],
        out_shape=[jax.ShapeDtypeStruct((rows, dil * ATTN_WIDTH), BF16),
                   jax.ShapeDtypeStruct((rows, dil * LANES), F32)],
        scratch_shapes=[pltpu.VMEM((TQ + 2 * HALF_WINDOW, ATTN_WIDTH), BF16),
                        pltpu.VMEM((TQ + 2 * HALF_WINDOW, ATTN_WIDTH), BF16)],
        compiler_params=_cparams("arbitrary", "arbitrary", "arbitrary"),
        name=f"attn_d{dil}",
    )(q, k, k, k, v, v, v)


def _shifted(prev, cur, nxt, shift):
    ext = jnp.concatenate([prev, cur, nxt], axis=0)
    n = ext.shape[0]
    lo = prev.shape[0]
    return pltpu.roll(ext, shift % n, axis=0)[lo:lo + cur.shape[0]]


def _token_order(o_ref, l_ref, oslab_ref, lslab_ref, dil):
    rows = TM // dil
    for r in range(dil):
        lslab_ref[pl.ds(r, rows, stride=dil), :] = l_ref[:, r * LANES:(r + 1) * LANES]
        for s in range(ATTN_SLABS):
            c0 = r * ATTN_WIDTH + s * LANES
            oslab_ref[s, pl.ds(r, rows, stride=dil), :] = o_ref[:, c0:c0 + LANES].astype(F32)
    return jnp.concatenate([oslab_ref[s] for s in range(ATTN_SLABS)], axis=1), lslab_ref[...]


def _outproj0_ffn_kernel(x_ref, p_ref, pp_ref, pn_ref, gb_ref, o1_ref, o4_ref, o16_ref, l1_ref, l4_ref, l16_ref,
                         cw_ref, ex_ref, w_ref, g_ref, wg_ref, wu_ref, wd_ref, out_ref,
                         os4_ref, ls4_ref, os16_ref, ls16_ref):
    tile = pl.program_id(0) % TILES_PER_SEQ
    first = (tile == 0)
    last = (tile == TILES_PER_SEQ - 1)
    cur = p_ref[...].astype(F32)
    prev = jnp.where(first, 0.0, pp_ref[...].astype(F32))
    nxt = jnp.where(last, 0.0, pn_ref[...].astype(F32))
    cw = cw_ref[...]
    conv = (cw[0:1, :] * _shifted(prev, cur, nxt, 1) + cw[1:2, :] * cur + cw[2:3, :] * _shifted(prev, cur, nxt, -1))
    ya = gb_ref[...].astype(F32) * conv

    o1, l1 = o1_ref[...].astype(F32), l1_ref[...]
    o2, l2 = _token_order(o4_ref, l4_ref, os4_ref, ls4_ref, 4)
    o3, l3 = _token_order(o16_ref, l16_ref, os16_ref, ls16_ref, 16)
    m = jnp.maximum(jnp.maximum(l1, l2), l3)
    e1, e2, e3 = jnp.exp(l1 - m), jnp.exp(l2 - m), jnp.exp(l3 - m)
    tot = e1 + e2 + e3
    yb = None
    for e, o in ((e1, o1), (e2, o2), (e3, o3)):
        wgt = e / tot
        hi = wgt.astype(BF16)
        lo = (wgt - hi.astype(F32)).astype(BF16)
        wide = jnp.dot(jnp.concatenate([hi, lo], axis=1), ex_ref[...], preferred_element_type=F32)
        term = wide * o
        yb = term if yb is None else yb + term
    y = jnp.concatenate([ya.astype(BF16), yb.astype(BF16)], axis=1)
    x = x_ref[...] + jnp.dot(y, w_ref[...], preferred_element_type=F32)
    out_ref[...] = _ffn_math(x, g_ref, wg_ref, wu_ref, wd_ref)


def _outproj0_ffn(x, p, gb, outs, lses, conv_w, expand, w, ffn_w):
    row = pl.BlockSpec((TM, D_MODEL), lambda i: (i, 0))
    half = pl.BlockSpec((TM, CONV_WIDTH), lambda i: (i, 0))
    hb = TM // BF16_ROWS
    nblk = T_TOK // BF16_ROWS
    prev = pl.BlockSpec((BF16_ROWS, CONV_WIDTH), lambda i: (jnp.maximum(i * hb - 1, 0), 0))
    nxt = pl.BlockSpec((BF16_ROWS, CONV_WIDTH), lambda i: (jnp.minimum((i + 1) * hb, nblk - 1), 0))
    o_specs = [pl.BlockSpec((TM // dil, dil * ATTN_WIDTH), lambda i: (i, 0)) for dil in DILATIONS]
    l_specs = [pl.BlockSpec((TM // dil, dil * LANES), lambda i: (i, 0)) for dil in DILATIONS]
    return pl.pallas_call(
        _outproj0_ffn_kernel,
        grid=(N_TILES,),
        in_specs=[row, half, prev, nxt, half] + o_specs + l_specs
                 + [_const_spec(conv_w.shape), _const_spec(expand.shape), _const_spec(w.shape)] + _ffn_weight_specs(),
        out_specs=row,
        out_shape=jax.ShapeDtypeStruct((T_TOK, D_MODEL), F32),
        scratch_shapes=[pltpu.VMEM((ATTN_SLABS, TM, LANES), F32), pltpu.VMEM((TM, LANES), F32),
                        pltpu.VMEM((ATTN_SLABS, TM, LANES), F32), pltpu.VMEM((TM, LANES), F32)],
        compiler_params=_cparams("arbitrary"),
        name="outproj0_ffn",
    )(x, p, p, p, gb, *outs, *lses, conv_w, expand, w, *ffn_w)


def _ffn_inproj1_kernel(x_ref, g_ref, wg_ref, wu_ref, wd_ref, g2_ref, w_ref, x_out_ref, xb_ref, gate_ref):
    x = _ffn_math(x_ref[...], g_ref, wg_ref, wu_ref, wd_ref)
    x_out_ref[...] = x
    h = _rmsnorm(x, g2_ref[...]).astype(BF16)
    z = jnp.dot(h, w_ref[...], preferred_element_type=F32)
    xb_ref[...] = z[:, :LRU_WIDTH]
    gate = z[:, LRU_WIDTH:]
    inner = math.sqrt(2.0 / math.pi) * (gate + 0.044715 * (gate * gate * gate))
    gate_ref[...] = (0.5 * gate * (1.0 + jnp.tanh(inner))).astype(BF16)


def _ffn_inproj1(x, ffn_w, g2, w):
    row = pl.BlockSpec((TM, D_MODEL), lambda i: (i, 0))
    return pl.pallas_call(
        _ffn_inproj1_kernel,
        grid=(N_TILES,),
        in_specs=[row] + _ffn_weight_specs() + [_const_spec((1, D_MODEL)), _const_spec(w.shape)],
        out_specs=[row, row, row],
        out_shape=[jax.ShapeDtypeStruct((T_TOK, D_MODEL), F32), jax.ShapeDtypeStruct((T_TOK, LRU_WIDTH), F32),
                   jax.ShapeDtypeStruct((T_TOK, LRU_WIDTH), BF16)],
        compiler_params=_cparams("arbitrary"),
        name="ffn_inproj1",
    )(x, *ffn_w, g2, w)


def _lru_terms(xb, wa_ref, ba_ref, wi_ref, bi_ref, lam_ref):
    xg = xb.astype(BF16)
    ra, ia = [], []
    for g in range(LRU_BLOCKS):
        blk = xg[:, g * LRU_BLOCK_DIM:(g + 1) * LRU_BLOCK_DIM]
        ra.append(jnp.dot(blk, wa_ref[g], preferred_element_type=F32))
        ia.append(jnp.dot(blk, wi_ref[g], preferred_element_type=F32))
    r = _sigmoid(jnp.concatenate(ra, axis=1) + ba_ref[...])
    i = _sigmoid(jnp.concatenate(ia, axis=1) + bi_ref[...])
    z = -lam_ref[...]
    softplus = jnp.maximum(z, 0.0) + jnp.log(1.0 + jnp.exp(-jnp.abs(z)))
    a = jnp.exp2(r * ((-LRU_C * LOG2E) * softplus))
    t = 1.0 - a * a
    root = jnp.where(t == 0.0, 0.0, t * lax.rsqrt(t))
    b = root * i * xb
    return a, b


def _chunk_rows(ref, j):
    return jnp.concatenate([ref[s, pl.ds(j, SUBLANES, stride=CHUNK_PITCH), :] for s in range(LRU_SLABS)], axis=1)


def _chunked_conv(cur_ref, prev_ref, next_ref, cw_ref, cb_ref, first, last, xs_ref):
    cur = cur_ref[...]
    prev = jnp.where(first, 0.0, prev_ref[...])
    nxt = jnp.where(last, 0.0, next_ref[...])
    for c in range(SUBLANES):
        follow = cur[(c + 1) * CHUNK:(c + 1) * CHUNK + SUBLANES] if c + 1 < SUBLANES else nxt
        for s in range(LRU_SLABS):
            lanes = slice(s * LANES, (s + 1) * LANES)
            xs_ref[s, c * CHUNK_PITCH:c * CHUNK_PITCH + CHUNK, :] = cur[c * CHUNK:(c + 1) * CHUNK, lanes]
            xs_ref[s, c * CHUNK_PITCH + CHUNK:(c + 1) * CHUNK_PITCH, :] = follow[:, lanes]
    x = {j: _chunk_rows(xs_ref, j) for j in range(CHUNK + 1)}
    sub = lax.broadcasted_iota(jnp.int32, (SUBLANES, LRU_WIDTH), 0)
    for back in (1, 2):
        x[-back] = jnp.where(sub == 0, prev[SUBLANES - back:SUBLANES - back + 1, :],
                             pltpu.roll(x[CHUNK - back], 1, axis=0))
    w = [jnp.broadcast_to(cw_ref[t:t + 1, :], (SUBLANES, LRU_WIDTH)) for t in range(4)]
    bias = jnp.broadcast_to(cb_ref[...], (SUBLANES, LRU_WIDTH))
    return jnp.concatenate(
        [w[0] * x[j - 2] + w[1] * x[j - 1] + w[2] * x[j] + w[3] * x[j + 1] + bias for j in range(CHUNK)], axis=0)


def _scan_tile(a, b, h_ref, carry, reverse):
    steps = range(CHUNK - 1, -1, -1) if reverse else range(CHUNK)
    local, prod = [None] * CHUNK, [None] * CHUNK
    h = p = None
    for j in steps:
        aj, bj = a[j * SUBLANES:(j + 1) * SUBLANES], b[j * SUBLANES:(j + 1) * SUBLANES]
        h = bj if h is None else aj * h + bj
        p = aj if p is None else aj * p
        local[j], prod[j] = h, p

    chunk_in = [None] * SUBLANES
    c_state = carry
    for c in (range(SUBLANES - 1, -1, -1) if reverse else range(SUBLANES)):
        chunk_in[c] = c_state
        c_state = h[c:c + 1, :] + p[c:c + 1, :] * c_state
    chunk_in = jnp.concatenate(chunk_in, axis=0)

    for j in range(CHUNK):
        full = local[j] + prod[j] * chunk_in
        for s in range(LRU_SLABS):
            h_ref[s, pl.ds(j, SUBLANES, stride=CHUNK_PITCH), :] = full[:, s * LANES:(s + 1) * LANES]
    out = jnp.concatenate(
        [jnp.concatenate([h_ref[s, c * CHUNK_PITCH:c * CHUNK_PITCH + CHUNK, :] for c in range(SUBLANES)], axis=0)
         for s in range(LRU_SLABS)], axis=1)
    return out, c_state


def _lru_kernel(xf_ref, xfp_ref, xfn_ref, xr_ref, xrp_ref, xrn_ref, cw_ref, cb_ref,
                fwa_ref, fba_ref, fwi_ref, fbi_ref, flam_ref, bwa_ref, bba_ref, bwi_ref, bbi_ref, blam_ref,
                hf_ref, hb_ref, cf_ref, cr_ref, xs_ref, h_ref):
    i = pl.program_id(1)

    @pl.when(i == 0)
    def _():
        cf_ref[...] = jnp.zeros_like(cf_ref)
        cr_ref[...] = jnp.zeros_like(cr_ref)

    def conv(cur_ref, prev_ref, next_ref, tile):
        return _chunked_conv(cur_ref, prev_ref, next_ref, cw_ref, cb_ref, tile == 0, tile == SCAN_TILES - 1, xs_ref)

    a, b = _lru_terms(conv(xf_ref, xfp_ref, xfn_ref, i), fwa_ref, fba_ref, fwi_ref, fbi_ref, flam_ref)
    h, carry = _scan_tile(a, b, h_ref, cf_ref[0:1, :], reverse=False)
    hf_ref[...] = h.astype(BF16)
    cf_ref[0:1, :] = carry

    a, b = _lru_terms(conv(xr_ref, xrp_ref, xrn_ref, SCAN_TILES - 1 - i), bwa_ref, bba_ref, bwi_ref, bbi_ref, blam_ref)
    h, carry = _scan_tile(a, b, h_ref, cr_ref[0:1, :], reverse=True)
    hb_ref[...] = h.astype(BF16)
    cr_ref[0:1, :] = carry


def _lru(xb, conv_w, conv_b, fwd, bwd):
    hb = TS // SUBLANES
    sblk = SEQ // SUBLANES
    fidx = lambda b, i: b * SCAN_TILES + i
    ridx = lambda b, i: b * SCAN_TILES + SCAN_TILES - 1 - i

    def specs(tile_of):
        local = lambda b, i: tile_of(b, i) - b * SCAN_TILES
        return [pl.BlockSpec((TS, LRU_WIDTH), lambda b, i: (tile_of(b, i), 0)),
                pl.BlockSpec((SUBLANES, LRU_WIDTH), lambda b, i: (b * sblk + jnp.maximum(local(b, i) * hb - 1, 0), 0)),
                pl.BlockSpec((SUBLANES, LRU_WIDTH),
                             lambda b, i: (b * sblk + jnp.minimum((local(b, i) + 1) * hb, sblk - 1), 0))]

    wspec = [_const_spec((LRU_BLOCKS, LRU_BLOCK_DIM, LRU_BLOCK_DIM)), _const_spec((1, LRU_WIDTH)),
             _const_spec((LRU_BLOCKS, LRU_BLOCK_DIM, LRU_BLOCK_DIM)), _const_spec((1, LRU_WIDTH)),
             _const_spec((1, LRU_WIDTH))]
    out = jax.ShapeDtypeStruct((T_TOK, LRU_WIDTH), BF16)
    scan_scratch = pltpu.VMEM((LRU_SLABS, SUBLANES * CHUNK_PITCH, LANES), F32)
    return pl.pallas_call(
        _lru_kernel,
        grid=(N_SEQ, SCAN_TILES),
        in_specs=specs(fidx) + specs(ridx) + [_const_spec(conv_w.shape), _const_spec((1, LRU_WIDTH))] + wspec + wspec,
        out_specs=[pl.BlockSpec((TS, LRU_WIDTH), lambda b, i: (fidx(b, i), 0)),
                   pl.BlockSpec((TS, LRU_WIDTH), lambda b, i: (ridx(b, i), 0))],
        out_shape=[out, out],
        scratch_shapes=[pltpu.VMEM((SUBLANES, LRU_WIDTH), F32), pltpu.VMEM((SUBLANES, LRU_WIDTH), F32),
                        scan_scratch, scan_scratch],
        compiler_params=_cparams("arbitrary", "arbitrary"),
        name="lru_scan",
    )(xb, xb, xb, xb, xb, xb, conv_w, conv_b, *fwd, *bwd)


def _outproj1_ffn_kernel(x_ref, hf_ref, hb_ref, gate_ref, w_ref, g_ref, wg_ref, wu_ref, wd_ref, fn_ref, out_ref):
    y = (hf_ref[...].astype(F32) + hb_ref[...].astype(F32)) * gate_ref[...].astype(F32)
    x = x_ref[...] + jnp.dot(y.astype(BF16), w_ref[...], preferred_element_type=F32)
    out_ref[...] = _rmsnorm(_ffn_math(x, g_ref, wg_ref, wu_ref, wd_ref), fn_ref[...])


def _outproj1_ffn(x, hf, hb, gate, w, ffn_w, fn, tile0, n_tiles):
    row = pl.BlockSpec((TM, D_MODEL), lambda i: (i + tile0, 0))
    return pl.pallas_call(
        _outproj1_ffn_kernel,
        grid=(n_tiles,),
        in_specs=[row, row, row, row, _const_spec(w.shape)] + _ffn_weight_specs() + [_const_spec((1, D_MODEL))],
        out_specs=pl.BlockSpec((TM, D_MODEL), lambda i: (i, 0)),
        out_shape=jax.ShapeDtypeStruct((n_tiles * TM, D_MODEL), F32),
        compiler_params=_cparams("arbitrary"),
        name="outproj1_ffn",
    )(x, hf, hb, gate, w, *ffn_w, fn)


def _rope_freq_lanes():
    inv_freq = ROPE_THETA ** (-jnp.arange(ROPE_HALF, dtype=F32) / ROPE_HALF)
    dim = jnp.arange(LANES) % HEAD_DIM
    return jnp.where(dim < ROPE_DIM, inv_freq[dim % ROPE_HALF], 0.0).astype(F32).reshape(1, LANES)


def _head_expand_matrix():
    src = jnp.arange(2 * LANES) % LANES
    dst = jnp.arange(ATTN_WIDTH) // HEAD_DIM
    return (src[:, None] == dst[None, :]).astype(BF16)


def kernel(x_prompt, x_sample, l0_ffn1_norm, l0_ffn1_w_gate, l0_ffn1_w_up, l0_ffn1_w_down, l0_mix_norm, l0_w_in, l0_conv_w, l0_w_out, l0_ffn2_norm, l0_ffn2_w_gate, l0_ffn2_w_up, l0_ffn2_w_down, l1_ffn1_norm, l1_ffn1_w_gate, l1_ffn1_w_up, l1_ffn1_w_down, l1_mix_norm, l1_w_in, l1_conv_w, l1_conv_b, l1_fwd_w_a, l1_fwd_b_a, l1_fwd_w_i, l1_fwd_b_i, l1_fwd_lambda, l1_bwd_w_a, l1_bwd_b_a, l1_bwd_w_i, l1_bwd_b_i, l1_bwd_lambda, l1_w_out, l1_ffn2_norm, l1_ffn2_w_gate, l1_ffn2_w_up, l1_ffn2_w_down, final_norm):
    vec = lambda t: t.reshape(1, -1).astype(F32)
    wt = lambda t: t.astype(BF16)
    ffn_w = lambda g, w_gate, w_up, w_down: (vec(g), wt(w_gate), wt(w_up), wt(w_down))
    x_parts = (x_prompt.reshape(-1, D_MODEL), x_sample.reshape(-1, D_MODEL))

    x = _ffn(x_parts, ffn_w(l0_ffn1_norm, l0_ffn1_w_gate, l0_ffn1_w_up, l0_ffn1_w_down))
    p, gb, *qkv = _inproj0(x, vec(l0_mix_norm), wt(l0_w_in), _rope_tables(_rope_freq_lanes()))
    branches = [_attention_branch(*qkv[3 * n:3 * n + 3], dil) for n, dil in enumerate(DILATIONS)]
    x = _outproj0_ffn(x, p, gb, [o for o, _ in branches], [l for _, l in branches], l0_conv_w.astype(F32),
                      _head_expand_matrix(), wt(l0_w_out),
                      ffn_w(l0_ffn2_norm, l0_ffn2_w_gate, l0_ffn2_w_up, l0_ffn2_w_down))

    x, xb, gate = _ffn_inproj1(x, ffn_w(l1_ffn1_norm, l1_ffn1_w_gate, l1_ffn1_w_up, l1_ffn1_w_down),
                               vec(l1_mix_norm), wt(l1_w_in))
    fwd = (wt(l1_fwd_w_a), vec(l1_fwd_b_a), wt(l1_fwd_w_i), vec(l1_fwd_b_i), vec(l1_fwd_lambda))
    bwd = (wt(l1_bwd_w_a), vec(l1_bwd_b_a), wt(l1_bwd_w_i), vec(l1_bwd_b_i), vec(l1_bwd_lambda))
    hf, hb = _lru(xb, l1_conv_w.astype(F32), vec(l1_conv_b), fwd, bwd)
    last = (x, hf, hb, gate, wt(l1_w_out), ffn_w(l1_ffn2_norm, l1_ffn2_w_gate, l1_ffn2_w_up, l1_ffn2_w_down),
            vec(final_norm))
    prompt_tiles = x_parts[0].shape[0] // TM
    y_prompt = _outproj1_ffn(*last, tile0=0, n_tiles=prompt_tiles)
    y_sample = _outproj1_ffn(*last, tile0=prompt_tiles, n_tiles=N_TILES - prompt_tiles)
    return (y_prompt.reshape(x_prompt.shape), y_sample.reshape(x_sample.shape))
```
